```python
import jax, jax.numpy as jnp
from jax import lax
import numpy as np

D_MODEL = 1024
BATCH = 8
SEQ = 2048
DEPTH = 1

PLE_DIM = 256
LRU_WIDTH = D_MODEL
LRU_BLOCKS = 8
LRU_BLOCK = LRU_WIDTH // LRU_BLOCKS
CONV_WIDTH = 4
LRU_C = 8.0
LRU_A_MIN = 0.9
LRU_A_MAX = 0.999
GLA_HEADS = 4
GLA_DK = D_MODEL // 2 // GLA_HEADS
GLA_DV = D_MODEL // GLA_HEADS
GLA_RANK = 16
GLA_TAU = 16.0
GLA_CHUNK = 64
N_EXPERTS = 32
TOP_K = 4
D_FF = D_MODEL
SWIGLU_LIMIT = 7.0
SWIGLU_ALPHA = 1.702
MOE_BLOCK = 128
LN_EPS = 1e-5
RMS_EPS = 1e-5
DEEPNORM_ALPHA = (2.0 * DEPTH) ** 0.25
DEEPNORM_BETA = (8.0 * DEPTH) ** -0.25

IN_WIDTHS = [LRU_WIDTH, LRU_WIDTH, GLA_HEADS * GLA_DK, GLA_HEADS * GLA_DK,
             GLA_HEADS * GLA_DV, GLA_HEADS * GLA_DV, GLA_RANK, D_MODEL, D_MODEL]
IN_TOTAL = int(sum(IN_WIDTHS))
IN_SPLITS = [int(s) for s in np.cumsum(IN_WIDTHS)[:-1]]

kernel_name = "hybrid_rglru_gla_moe_deepnorm_ple"


def layer_norm(x, g, b):
    xf = x.astype(jnp.float32)
    mu = jnp.mean(xf, axis=-1, keepdims=True)
    var = jnp.mean(jnp.square(xf - mu), axis=-1, keepdims=True)
    return ((xf - mu) * lax.rsqrt(var + LN_EPS) * g + b).astype(x.dtype)


def causal_depthwise_conv(x, w, b):
    s = x.shape[1]
    xp = jnp.pad(x, ((0, 0), (CONV_WIDTH - 1, 0), (0, 0)))
    out = xp[:, 0:s] * w[0]
    for k in range(1, CONV_WIDTH):
        out = out + xp[:, k:k + s] * w[k]
    return out + b


def rg_lru(x, w_r, b_r, w_i, b_i, lam):
    bsz, s, c = x.shape
    xb = x.reshape(bsz, s, LRU_BLOCKS, LRU_BLOCK)
    r = jax.nn.sigmoid(jnp.einsum('bsgc,gcd->bsgd', xb, w_r).reshape(bsz, s, c) + b_r)
    i = jax.nn.sigmoid(jnp.einsum('bsgc,gcd->bsgd', xb, w_i).reshape(bsz, s, c) + b_i)
    log_a = (-LRU_C * r * jax.nn.softplus(-lam)).astype(jnp.float32)
    a = jnp.exp(log_a)
    mult = jnp.sqrt(-jnp.expm1(2.0 * log_a))
    u = mult * (i * x).astype(jnp.float32)

    def combine(left, right):
        a1, b1 = left
        a2, b2 = right
        return a1 * a2, a2 * b1 + b2

    _, h = lax.associative_scan(combine, (a, u), axis=1)
    return h.astype(x.dtype)


def gla_chunked(q, k, v, log_g):
    bsz, s, h, dk = q.shape
    dv = v.shape[-1]
    n = s // GLA_CHUNK

    def to_chunks(t):
        t = t.astype(jnp.float32).reshape(bsz, n, GLA_CHUNK, h, t.shape[-1])
        return t.transpose(1, 0, 3, 2, 4)

    qc = to_chunks(q * (dk ** -0.5))
    kc, vc, gc = to_chunks(k), to_chunks(v), to_chunks(log_g)
    causal = jnp.tril(jnp.ones((GLA_CHUNK, GLA_CHUNK), dtype=bool))

    def step(state, inp):
        qn, kn, vn, gn = inp
        bcum = jnp.cumsum(gn, axis=2)
        o_inter = jnp.einsum('bhik,bhkv->bhiv', qn * jnp.exp(bcum), state)
        diff = bcum[:, :, :, None, :] - bcum[:, :, None, :, :]
        decay = jnp.exp(jnp.where(causal[:, :, None], diff, -jnp.inf))
        scores = jnp.einsum('bhijk,bhjk->bhij', qn[:, :, :, None, :] * decay, kn)
        o_intra = jnp.einsum('bhij,bhjv->bhiv', scores, vn)
        b_last = bcum[:, :, -1:, :]
        k_dec = kn * jnp.exp(b_last - bcum)
        state = (jnp.exp(b_last[:, :, 0, :])[..., None] * state
                 + jnp.einsum('bhjk,bhjv->bhkv', k_dec, vn))
        return state, o_inter + o_intra

    state0 = jnp.zeros((bsz, h, dk, dv), jnp.float32)
    _, o = lax.scan(step, state0, (qc, kc, vc, gc))
    o = o.transpose(1, 0, 3, 2, 4).reshape(bsz, s, h, dv)
    return o


def moe(x, w_router, b_router, w_up, b_up, w_down, b_down):
    bsz, s, d = x.shape
    t = bsz * s
    xt = x.reshape(t, d)
    logits = (xt @ w_router + b_router).astype(jnp.float32)
    top_val, top_idx = lax.top_k(logits, TOP_K)
    gates = jax.nn.softmax(top_val, axis=-1).astype(x.dtype)

    n_assign = t * TOP_K
    expert_flat = top_idx.reshape(n_assign)
    token_flat = (jnp.arange(n_assign, dtype=jnp.int32) // TOP_K).astype(jnp.int32)
    gate_flat = gates.reshape(n_assign)
    order = jnp.argsort(expert_flat)
    sorted_expert = expert_flat[order]
    counts = jnp.bincount(expert_flat, length=N_EXPERTS)
    start = jnp.cumsum(counts) - counts
    padded_counts = (counts + MOE_BLOCK - 1) // MOE_BLOCK * MOE_BLOCK
    padded_end = jnp.cumsum(padded_counts)
    padded_start = padded_end - padded_counts
    slot = padded_start[sorted_expert] + jnp.arange(n_assign) - start[sorted_expert]

    n_slots = n_assign + N_EXPERTS * MOE_BLOCK
    n_blocks = n_slots // MOE_BLOCK
    slot_token = jnp.full((n_slots,), t, jnp.int32).at[slot].set(token_flat[order])
    slot_gate = jnp.zeros((n_slots,), x.dtype).at[slot].set(gate_flat[order])
    block_expert = jnp.minimum(
        jnp.searchsorted(padded_end, jnp.arange(n_blocks) * MOE_BLOCK, side='right'),
        N_EXPERTS - 1)
    x_pad = jnp.concatenate([xt, jnp.zeros((1, d), xt.dtype)], axis=0)

    def expert_block(args):
        tok, g, e = args
        xb = x_pad[tok]
        hdn = xb @ w_up[e] + b_up[e]
        gate_h = jnp.minimum(hdn[:, :D_FF], SWIGLU_LIMIT)
        up_h = jnp.clip(hdn[:, D_FF:], -SWIGLU_LIMIT, SWIGLU_LIMIT)
        act = (up_h + 1.0) * gate_h * jax.nn.sigmoid(SWIGLU_ALPHA * gate_h)
        y = act @ w_down[e] + b_down[e]
        return y * g[:, None]

    y_slots = lax.map(expert_block, (slot_token.reshape(n_blocks, MOE_BLOCK),
                                     slot_gate.reshape(n_blocks, MOE_BLOCK),
                                     block_expert))
    out = jnp.zeros((t + 1, d), y_slots.dtype).at[slot_token].add(y_slots.reshape(n_slots, d))
    return out[:t].reshape(bsz, s, d)


def setup_inputs(seed: int = 0) -> dict:
    key = jax.random.key(seed)
    ks = jax.random.split(key, 27)
    f32 = jnp.float32
    L = DEPTH

    def nrm(k, shape, scale):
        return jax.random.normal(k, shape, f32) * scale

    u = jax.random.uniform(ks[9], (L, LRU_WIDTH), f32, LRU_A_MIN, LRU_A_MAX)
    s_base = u ** (1.0 / LRU_C)
    lru_lambda = jnp.log(s_base) - jnp.log1p(-s_base)
    return {
        "x": nrm(ks[0], (BATCH, SEQ, D_MODEL), 1.0),
        "p": nrm(ks[1], (DEPTH, BATCH, SEQ, PLE_DIM), 1.0),
        "w_in": nrm(ks[2], (L, D_MODEL, IN_TOTAL), D_MODEL ** -0.5),
        "conv_w": nrm(ks[3], (L, CONV_WIDTH, LRU_WIDTH), CONV_WIDTH ** -0.5),
        "conv_b": nrm(ks[4], (L, LRU_WIDTH), 0.01),
        "lru_w_r": nrm(ks[5], (L, LRU_BLOCKS, LRU_BLOCK, LRU_BLOCK), LRU_BLOCK ** -0.5),
        "lru_b_r": nrm(ks[6], (L, LRU_WIDTH), 0.01),
        "lru_w_i": nrm(ks[7], (L, LRU_BLOCKS, LRU_BLOCK, LRU_BLOCK), LRU_BLOCK ** -0.5),
        "lru_b_i": nrm(ks[8], (L, LRU_WIDTH), 0.01),
        "lru_lambda": lru_lambda,
        "gla_w_gate": nrm(ks[10], (L, GLA_RANK, GLA_HEADS * GLA_DK), GLA_RANK ** -0.5),
        "gla_b_gate": nrm(ks[11], (L, GLA_HEADS * GLA_DK), 0.01),
        "gla_norm_g": 1.0 + nrm(ks[12], (L, GLA_DV), 0.02),
        "w_out": nrm(ks[13], (L, D_MODEL, D_MODEL), DEEPNORM_BETA * D_MODEL ** -0.5),
        "ln1_g": 1.0 + nrm(ks[14], (L, D_MODEL), 0.02),
        "ln1_b": nrm(ks[15], (L, D_MODEL), 0.01),
        "w_router": nrm(ks[16], (L, D_MODEL, N_EXPERTS), D_MODEL ** -0.5),
        "b_router": nrm(ks[17], (L, N_EXPERTS), 0.01),
        "w_up": nrm(ks[18], (L, N_EXPERTS, D_MODEL, 2 * D_FF), D_MODEL ** -0.5),
        "b_up": nrm(ks[19], (L, N_EXPERTS, 2 * D_FF), 0.01),
        "w_down": nrm(ks[20], (L, N_EXPERTS, D_FF, D_MODEL), DEEPNORM_BETA * D_FF ** -0.5),
        "b_down": nrm(ks[21], (L, N_EXPERTS, D_MODEL), 0.01),
        "ln2_g": 1.0 + nrm(ks[22], (L, D_MODEL), 0.02),
        "ln2_b": nrm(ks[23], (L, D_MODEL), 0.01),
        "w_ple": nrm(ks[24], (L, PLE_DIM, D_MODEL), PLE_DIM ** -0.5),
        "w_ple_gate": nrm(ks[25], (L, D_MODEL, D_MODEL), D_MODEL ** -0.5),
        "b_ple_gate": nrm(ks[26], (L, D_MODEL), 0.01),
    }


def reference(x, p, w_in, conv_w, conv_b, lru_w_r, lru_b_r, lru_w_i, lru_b_i, lru_lambda,
              gla_w_gate, gla_b_gate, gla_norm_g, w_out, ln1_g, ln1_b,
              w_router, b_router, w_up, b_up, w_down, b_down, ln2_g, ln2_b,
              w_ple, w_ple_gate, b_ple_gate):
    bsz, s, d = x.shape
    h = x
    for i in range(DEPTH):
        proj = h @ w_in[i]
        xa, ga, q, k, v, go, glr, ma, mb = jnp.split(proj, IN_SPLITS, axis=-1)

        xa = causal_depthwise_conv(xa, conv_w[i], conv_b[i])
        ya = rg_lru(xa, lru_w_r[i], lru_b_r[i], lru_w_i[i], lru_b_i[i], lru_lambda[i])
        ya = ya * jax.nn.gelu(ga)

        log_g = jax.nn.log_sigmoid(glr @ gla_w_gate[i] + gla_b_gate[i]) / GLA_TAU
        o = gla_chunked(q.reshape(bsz, s, GLA_HEADS, GLA_DK),
                        k.reshape(bsz, s, GLA_HEADS, GLA_DK),
                        v.reshape(bsz, s, GLA_HEADS, GLA_DV),
                        log_g.reshape(bsz, s, GLA_HEADS, GLA_DK))
        o = o * lax.rsqrt(jnp.mean(jnp.square(o), axis=-1, keepdims=True) + RMS_EPS) * gla_norm_g[i]
        yb = o.reshape(bsz, s, GLA_HEADS * GLA_DV).astype(h.dtype) * jax.nn.silu(go)

        y = jax.nn.sigmoid(ma) * ya + jax.nn.sigmoid(mb) * yb
        h = layer_norm(DEEPNORM_ALPHA * h + y @ w_out[i], ln1_g[i], ln1_b[i])

        m = moe(h, w_router[i], b_router[i], w_up[i], b_up[i], w_down[i], b_down[i])
        h = layer_norm(DEEPNORM_ALPHA * h + m, ln2_g[i], ln2_b[i])

        h = h + jax.nn.sigmoid(h @ w_ple_gate[i] + b_ple_gate[i]) * (p[i] @ w_ple[i])
    return h
```

```python
import functools

import jax
import jax.numpy as jnp
from jax import lax
from jax.experimental import pallas as pl
from jax.experimental.pallas import tpu as pltpu

F32 = jnp.float32
BF16 = jnp.bfloat16

D_MODEL = 1024
PLE_DIM = 256
LRU_BLOCKS = 8
LRU_BLOCK = 128
CONV_WIDTH = 4
LRU_C = 8.0
GLA_HEADS = 4
GLA_DK = 128
GLA_DV = 256
GLA_RANK = 16
GLA_TAU = 16.0
GLA_CHUNK = 64
N_EXPERTS = 32
TOP_K = 4
D_FF = 1024
SWIGLU_LIMIT = 7.0
SWIGLU_ALPHA = 1.702
LN_EPS = 1e-5
RMS_EPS = 1e-5
DEEPNORM_ALPHA = 2.0 ** 0.25

LANES = 128
SUBLANES = 8
VMEM_LIMIT = 56 * 1024 * 1024

QK = GLA_HEADS * GLA_DK
VW = GLA_HEADS * GLA_DV
OFF_XA = 0
OFF_GA = OFF_XA + D_MODEL
OFF_Q = OFF_GA + D_MODEL
OFF_K = OFF_Q + QK
OFF_V = OFF_K + QK
OFF_GO = OFF_V + VW
OFF_GLR = OFF_GO + VW
OFF_MA = OFF_GLR + LANES
OFF_MB = OFF_MA + D_MODEL
IN_PACKED = OFF_MB + D_MODEL

TS = 256
TR = 512
BLK = 256
TC = 256


def _dot(a, b):
    return jnp.dot(a, b, preferred_element_type=F32)


def _softplus(x):
    return jnp.maximum(x, 0.0) + jnp.log1p(jnp.exp(-jnp.abs(x)))


def _layer_norm(x, g, b):
    mu = jnp.mean(x, axis=-1, keepdims=True)
    xc = x - mu
    var = jnp.mean(xc * xc, axis=-1, keepdims=True)
    return xc * lax.rsqrt(var + LN_EPS) * g + b


def _mixer_kernel(x_ref, win_ref, convw_ref, convb_ref, wri_ref, br_ref, bi_ref, lam_ref,
                  wg_ref, bg_ref, gnorm_ref, wout_ref, ln1g_ref, ln1b_ref, wrt_ref, brt_ref,
                  h1_ref, logits_ref,
                  xa_buf, a_buf, u_buf, y_buf, q_buf, k_buf, v_buf, lg_buf, o_buf, hc_ref, st_ref):
    s = pl.program_id(1)

    @pl.when(s == 0)
    def _():
        xa_buf[0:SUBLANES, :] = jnp.zeros((SUBLANES, D_MODEL), F32)
        hc_ref[...] = jnp.zeros_like(hc_ref)
        st_ref[...] = jnp.zeros_like(st_ref)

    xb = x_ref[...].astype(BF16)

    def proj(off, width):
        return _dot(xb, win_ref[:, off:off + width])

    xa_buf[SUBLANES:SUBLANES + TS, :] = proj(OFF_XA, D_MODEL)
    base = SUBLANES - (CONV_WIDTH - 1)
    xc = xa_buf[base:base + TS, :] * convw_ref[0:1, :]
    for kk in range(1, CONV_WIDTH):
        xc = xc + xa_buf[base + kk:base + kk + TS, :] * convw_ref[kk:kk + 1, :]
    xc = xc + convb_ref[...]
    xa_buf[0:SUBLANES, :] = xa_buf[TS:TS + SUBLANES, :]

    neg_c_sp = -LRU_C * _softplus(-lam_ref[...])
    for g in range(LRU_BLOCKS):
        sl = slice(g * LRU_BLOCK, (g + 1) * LRU_BLOCK)
        xg = xc[:, sl]
        ri = _dot(xg.astype(BF16), wri_ref[g])
        r = jax.nn.sigmoid(ri[:, :LRU_BLOCK] + br_ref[:, sl])
        i = jax.nn.sigmoid(ri[:, LRU_BLOCK:] + bi_ref[:, sl])
        log_a = neg_c_sp[:, sl] * r
        a = jnp.exp(log_a)
        a_buf[:, sl] = a
        u_buf[:, sl] = jnp.sqrt(jnp.tanh(-log_a) * (a * a + 1.0)) * (i * xg)

    def scan_body(t, h):
        h = a_buf[pl.ds(t, 1), :] * h + u_buf[pl.ds(t, 1), :]
        u_buf[pl.ds(t, 1), :] = h
        return h

    hc_ref[...] = lax.fori_loop(0, TS, scan_body, hc_ref[...], unroll=8)

    ga = proj(OFF_GA, D_MODEL)
    ma = proj(OFF_MA, D_MODEL)
    y_buf[...] = jax.nn.sigmoid(ma) * (u_buf[...] * jax.nn.gelu(ga))

    q_buf[...] = proj(OFF_Q, QK) * (GLA_DK ** -0.5)
    k_buf[...] = proj(OFF_K, QK)
    v_buf[...] = proj(OFF_V, VW)
    glr = proj(OFF_GLR, LANES)
    z = _dot(glr.astype(BF16), wg_ref[...]) + bg_ref[...]
    lg_buf[...] = (jnp.minimum(z, 0.0) - jnp.log1p(jnp.exp(-jnp.abs(z)))) * (1.0 / GLA_TAU)

    rows = lax.broadcasted_iota(jnp.int32, (GLA_CHUNK, GLA_CHUNK), 0)
    cols = lax.broadcasted_iota(jnp.int32, (GLA_CHUNK, GLA_CHUNK), 1)
    causal = rows >= cols
    tri = causal.astype(BF16)
    mid = GLA_CHUNK // 2 - 1

    def chunk_body(c, carry):
        r0 = pl.multiple_of(c * GLA_CHUNK, GLA_CHUNK)
        lg = lg_buf[pl.ds(r0, GLA_CHUNK), :]
        lg_hi = lg.astype(BF16)
        rem = lg - lg_hi.astype(F32)
        lg_mid = rem.astype(BF16)
        lg_lo = (rem - lg_mid.astype(F32)).astype(BF16)
        bcum = _dot(tri, lg_hi) + _dot(tri, lg_mid) + _dot(tri, lg_lo)
        for h in range(GLA_HEADS):
            ks = slice(h * GLA_DK, (h + 1) * GLA_DK)
            vs = slice(h * GLA_DV, (h + 1) * GLA_DV)
            b = bcum[:, ks]
            qh = q_buf[pl.ds(r0, GLA_CHUNK), ks]
            kh = k_buf[pl.ds(r0, GLA_CHUNK), ks]
            vh = v_buf[pl.ds(r0, GLA_CHUNK), vs].astype(BF16)
            b_last = b[GLA_CHUNK - 1:GLA_CHUNK, :]
            b_mid = b[mid:mid + 1, :]
            st = st_ref[h]
            q_in = (qh * jnp.exp(b)).astype(BF16)
            o_inter = lax.dot_general(q_in, st.astype(BF16), (((1,), (1,)), ((), ())),
                                      preferred_element_type=F32)
            q_t = (qh * jnp.exp(b - b_mid)).astype(BF16)
            k_t = (kh * jnp.exp(b_mid - b)).astype(BF16)
            scores = lax.dot_general(q_t, k_t, (((1,), (1,)), ((), ())), preferred_element_type=F32)
            scores = jnp.where(causal, scores, 0.0).astype(BF16)
            o = o_inter + _dot(scores, vh)
            k_dec = (kh * jnp.exp(b_last - b)).astype(BF16)
            st_ref[h] = st * jnp.exp(b_last) + lax.dot_general(
                vh, k_dec, (((0,), (0,)), ((), ())), preferred_element_type=F32)
            ms = jnp.mean(o * o, axis=-1, keepdims=True)
            o_buf[pl.ds(r0, GLA_CHUNK), vs] = o * lax.rsqrt(ms + RMS_EPS) * gnorm_ref[...]
        return carry

    lax.fori_loop(0, TS // GLA_CHUNK, chunk_body, 0)

    go = proj(OFF_GO, VW)
    mb = proj(OFF_MB, D_MODEL)
    y = y_buf[...] + jax.nn.sigmoid(mb) * (o_buf[...] * (go * jax.nn.sigmoid(go)))

    hpre = DEEPNORM_ALPHA * x_ref[...] + _dot(y.astype(BF16), wout_ref[...])
    h1 = _layer_norm(hpre, ln1g_ref[...], ln1b_ref[...])
    h1_ref[...] = h1
    logits_ref[...] = jnp.dot(h1, wrt_ref[...], preferred_element_type=F32,
                              precision=lax.Precision.HIGHEST) + brt_ref[...]


def _mixer(x, w_cat, conv_w, conv_b, w_ri, b_r, b_i, lam, w_g, b_g, gnorm, w_out, ln1_g, ln1_b, w_rt, b_rt):
    bsz, seq, d = x.shape
    const2 = lambda shape: pl.BlockSpec(shape, lambda b, s: (0, 0))
    const3 = lambda shape: pl.BlockSpec(shape, lambda b, s: (0, 0, 0))
    return pl.pallas_call(
        _mixer_kernel,
        grid=(bsz, seq // TS),
        in_specs=[
            pl.BlockSpec((None, TS, d), lambda b, s: (b, s, 0)),
            const2((d, IN_PACKED)),
            const2((CONV_WIDTH, d)), const2((1, d)),
            const3((LRU_BLOCKS, LRU_BLOCK, 2 * LRU_BLOCK)), const2((1, d)), const2((1, d)), const2((1, d)),
            const2((LANES, QK)), const2((1, QK)), const2((1, GLA_DV)),
            const2((d, d)), const2((1, d)), const2((1, d)),
            const2((d, LANES)), const2((1, LANES)),
        ],
        out_specs=[
            pl.BlockSpec((None, TS, d), lambda b, s: (b, s, 0)),
            pl.BlockSpec((None, TS, LANES), lambda b, s: (b, s, 0)),
        ],
        out_shape=[
            jax.ShapeDtypeStruct((bsz, seq, d), F32),
            jax.ShapeDtypeStruct((bsz, seq, LANES), F32),
        ],
        scratch_shapes=[
            pltpu.VMEM((TS + SUBLANES, d), F32),
            pltpu.VMEM((TS, d), F32),
            pltpu.VMEM((TS, d), F32),
            pltpu.VMEM((TS, d), F32),
            pltpu.VMEM((TS, QK), F32),
            pltpu.VMEM((TS, QK), F32),
            pltpu.VMEM((TS, VW), F32),
            pltpu.VMEM((TS, QK), F32),
            pltpu.VMEM((TS, VW), F32),
            pltpu.VMEM((1, d), F32),
            pltpu.VMEM((GLA_HEADS, GLA_DV, GLA_DK), F32),
        ],
        compiler_params=pltpu.CompilerParams(
            dimension_semantics=("arbitrary", "arbitrary"), vmem_limit_bytes=VMEM_LIMIT),
        name="mixer",
    )(x, w_cat, conv_w, conv_b, w_ri, b_r, b_i, lam, w_g, b_g, gnorm, w_out, ln1_g, ln1_b, w_rt, b_rt)


def _router_kernel(logits_ref, eidx_ref, gate_ref, rank_ref, cnt_ref, carry_ref):
    i = pl.program_id(0)

    @pl.when(i == 0)
    def _():
        carry_ref[...] = jnp.zeros_like(carry_ref)

    lane = lax.broadcasted_iota(jnp.int32, (TR, LANES), 1)
    work = logits_ref[...]
    vals, idxs, sels = [], [], []
    for _ in range(TOP_K):
        m = jnp.max(work, axis=-1, keepdims=True)
        idx = jnp.min(jnp.where(work == m, lane, LANES), axis=-1, keepdims=True)
        sel = lane == idx
        vals.append(m)
        idxs.append(idx)
        sels.append(sel)
        work = jnp.where(sel, -jnp.inf, work)

    es = [jnp.exp(v - vals[0]) for v in vals]
    denom = es[0] + es[1] + es[2] + es[3]

    member = jnp.zeros((TR, LANES), F32)
    for sel in sels:
        member = member + sel.astype(F32)
    rows = lax.broadcasted_iota(jnp.int32, (TR, TR), 0)
    cols = lax.broadcasted_iota(jnp.int32, (TR, TR), 1)
    strict = (rows > cols).astype(BF16)
    before = _dot(strict, member.astype(BF16)) + carry_ref[0:1, :]

    eidx = jnp.zeros((TR, LANES), jnp.int32)
    gate = jnp.zeros((TR, LANES), F32)
    rank = jnp.zeros((TR, LANES), jnp.int32)
    for kk in range(TOP_K):
        rk = jnp.sum(jnp.where(sels[kk], before, 0.0), axis=-1, keepdims=True).astype(jnp.int32)
        eidx = jnp.where(lane == kk, idxs[kk], eidx)
        gate = jnp.where(lane == kk, es[kk] / denom, gate)
        rank = jnp.where(lane == kk, rk, rank)
    eidx_ref[...] = eidx
    gate_ref[...] = gate
    rank_ref[...] = rank

    total = carry_ref[0:1, :] + jnp.sum(member, axis=0, keepdims=True)
    carry_ref[...] = jnp.broadcast_to(total, carry_ref.shape)
    cnt_ref[...] = jnp.broadcast_to(total, cnt_ref.shape).astype(jnp.int32)


def _router(logits):
    t = logits.shape[0]
    tile = pl.BlockSpec((TR, LANES), lambda i: (i, 0))
    return pl.pallas_call(
        _router_kernel,
        grid=(t // TR,),
        in_specs=[tile],
        out_specs=[tile, tile, tile, pl.BlockSpec((SUBLANES, LANES), lambda i: (0, 0))],
        out_shape=[
            jax.ShapeDtypeStruct((t, LANES), jnp.int32),
            jax.ShapeDtypeStruct((t, LANES), F32),
            jax.ShapeDtypeStruct((t, LANES), jnp.int32),
            jax.ShapeDtypeStruct((SUBLANES, LANES), jnp.int32),
        ],
        scratch_shapes=[pltpu.VMEM((SUBLANES, LANES), F32)],
        compiler_params=pltpu.CompilerParams(dimension_semantics=("arbitrary",)),
        name="router",
    )(logits)


def _row_gather_copy(src_hbm, row, dst, dst_row, sem):
    return pltpu.make_async_copy(src_hbm.at[pl.ds(row, 1), :], dst.at[pl.ds(dst_row, 1), :], sem)


def _expert_kernel(be_ref, nused_ref, tok_ref,
                   h1_hbm, gate_ref, wup_ref, bup_ref, wdn_ref, bdn_ref,
                   y_ref,
                   xbuf, sems, wup_bf, wdn_bf):
    i = pl.program_id(0)
    n_used = nused_ref[0]
    slot = i % 2

    def issue(block, dst_slot):
        def body(r, carry):
            tok = tok_ref[block * BLK + r]
            _row_gather_copy(h1_hbm, tok, xbuf.at[dst_slot], r, sems.at[dst_slot]).start()
            return carry
        lax.fori_loop(0, BLK, body, 0, unroll=8)

    @pl.when(i == 0)
    def _():
        issue(0, 0)

    @pl.when(i + 1 < n_used)
    def _():
        issue(i + 1, 1 - slot)

    first_of_expert = jnp.logical_or(i == 0, be_ref[i] != be_ref[jnp.maximum(i - 1, 0)])

    @pl.when(jnp.logical_and(i < n_used, first_of_expert))
    def _():
        wup_bf[...] = wup_ref[...].astype(BF16)
        wdn_bf[...] = wdn_ref[...].astype(BF16)

    @pl.when(i < n_used)
    def _():
        pltpu.make_async_copy(h1_hbm.at[pl.ds(0, BLK), :], xbuf.at[slot], sems.at[slot]).wait()
        xb = xbuf[slot].astype(BF16)
        hdn = _dot(xb, wup_bf[...]) + bup_ref[...]
        gate_h = jnp.minimum(hdn[:, :D_FF], SWIGLU_LIMIT)
        up_h = jnp.clip(hdn[:, D_FF:], -SWIGLU_LIMIT, SWIGLU_LIMIT)
        act = (up_h + 1.0) * gate_h * jax.nn.sigmoid(SWIGLU_ALPHA * gate_h)
        y = _dot(act.astype(BF16), wdn_bf[...]) + bdn_ref[...]
        y_ref[...] = y * gate_ref[...]

    @pl.when(i >= n_used)
    def _():
        y_ref[...] = jnp.zeros_like(y_ref)


def _experts(block_expert, n_used, slot_token, h1, slot_gate, w_up, b_up, w_down, b_down):
    n_slots = slot_token.shape[0]
    n_blocks = n_slots // BLK
    d = h1.shape[1]
    grid_spec = pltpu.PrefetchScalarGridSpec(
        num_scalar_prefetch=3,
        grid=(n_blocks,),
        in_specs=[
            pl.BlockSpec(memory_space=pl.ANY),
            pl.BlockSpec((BLK, 1), lambda i, be, nu, tk: (i, 0)),
            pl.BlockSpec((None, d, 2 * D_FF), lambda i, be, nu, tk: (be[i], 0, 0)),
            pl.BlockSpec((None, 1, 2 * D_FF), lambda i, be, nu, tk: (be[i], 0, 0)),
            pl.BlockSpec((None, D_FF, d), lambda i, be, nu, tk: (be[i], 0, 0)),
            pl.BlockSpec((None, 1, d), lambda i, be, nu, tk: (be[i], 0, 0)),
        ],
        out_specs=pl.BlockSpec((BLK, d), lambda i, be, nu, tk: (i, 0)),
        scratch_shapes=[
            pltpu.VMEM((2, BLK, d), F32),
            pltpu.SemaphoreType.DMA((2,)),
            pltpu.VMEM((d, 2 * D_FF), BF16),
            pltpu.VMEM((D_FF, d), BF16),
        ],
    )
    return pl.pallas_call(
        _expert_kernel,
        grid_spec=grid_spec,
        out_shape=jax.ShapeDtypeStruct((n_slots, d), F32),
        compiler_params=pltpu.CompilerParams(
            dimension_semantics=("arbitrary",), vmem_limit_bytes=VMEM_LIMIT),
        name="experts",
    )(block_expert, n_used, slot_token, h1, slot_gate, w_up, b_up, w_down, b_down)


def _final_kernel(pos_ref, y_hbm, h1_ref, p_ref, ln2g_ref, ln2b_ref, wple_ref, wpg_ref, bpg_ref,
                  out_ref, ybuf, sems):
    i = pl.program_id(0)
    n = pl.num_programs(0)
    slot = i % 2

    def issue(tile, dst_slot):
        def body(r, carry):
            for kk in range(TOP_K):
                row = pos_ref[(tile * TC + r) * TOP_K + kk]
                _row_gather_copy(y_hbm, row, ybuf.at[dst_slot, kk], r, sems.at[dst_slot]).start()
            return carry
        lax.fori_loop(0, TC, body, 0, unroll=4)

    @pl.when(i == 0)
    def _():
        issue(0, 0)

    @pl.when(i + 1 < n)
    def _():
        issue(i + 1, 1 - slot)

    for kk in range(TOP_K):
        pltpu.make_async_copy(y_hbm.at[pl.ds(0, TC), :], ybuf.at[slot, kk], sems.at[slot]).wait()
    m = (ybuf[slot, 0] + ybuf[slot, 1]) + (ybuf[slot, 2] + ybuf[slot, 3])
    h2 = _layer_norm(DEEPNORM_ALPHA * h1_ref[...] + m, ln2g_ref[...], ln2b_ref[...])
    gate = jax.nn.sigmoid(_dot(h2.astype(BF16), wpg_ref[...]) + bpg_ref[...])
    out_ref[...] = h2 + gate * _dot(p_ref[...].astype(BF16), wple_ref[...])


def _final(pos, y_slots, h1, p, ln2_g, ln2_b, w_ple, w_pg, b_pg):
    t, d = h1.shape
    const = lambda shape: pl.BlockSpec(shape, lambda i, ps: (0, 0))
    grid_spec = pltpu.PrefetchScalarGridSpec(
        num_scalar_prefetch=1,
        grid=(t // TC,),
        in_specs=[
            pl.BlockSpec(memory_space=pl.ANY),
            pl.BlockSpec((TC, d), lambda i, ps: (i, 0)),
            pl.BlockSpec((TC, PLE_DIM), lambda i, ps: (i, 0)),
            const((1, d)), const((1, d)),
            const((PLE_DIM, d)), const((d, d)), const((1, d)),
        ],
        out_specs=pl.BlockSpec((TC, d), lambda i, ps: (i, 0)),
        scratch_shapes=[
            pltpu.VMEM((2, TOP_K, TC, d), F32),
            pltpu.SemaphoreType.DMA((2,)),
        ],
    )
    return pl.pallas_call(
        _final_kernel,
        grid_spec=grid_spec,
        out_shape=jax.ShapeDtypeStruct((t, d), F32),
        compiler_params=pltpu.CompilerParams(
            dimension_semantics=("arbitrary",), vmem_limit_bytes=VMEM_LIMIT),
        name="final",
    )(pos, y_slots, h1, p, ln2_g, ln2_b, w_ple, w_pg, b_pg)


def kernel(x, p, w_in, conv_w, conv_b, lru_w_r, lru_b_r, lru_w_i, lru_b_i, lru_lambda, gla_w_gate, gla_b_gate, gla_norm_g, w_out, ln1_g, ln1_b, w_router, b_router, w_up, b_up, w_down, b_down, ln2_g, ln2_b, w_ple, w_ple_gate, b_ple_gate):
    bsz, seq, d = x.shape
    t = bsz * seq
    h = x
    for li in range(w_in.shape[0]):
        row = lambda v: v[li].reshape(1, -1)
        w = w_in[li]
        glr_end = OFF_GLR + GLA_RANK
        w_cat = jnp.concatenate(
            [w[:, :glr_end], jnp.zeros((d, LANES - GLA_RANK), w.dtype), w[:, glr_end:]],
            axis=1).astype(BF16)
        w_ri = jnp.concatenate([lru_w_r[li], lru_w_i[li]], axis=-1).astype(BF16)
        w_g = jnp.concatenate(
            [gla_w_gate[li], jnp.zeros((LANES - GLA_RANK, QK), F32)], axis=0).astype(BF16)
        w_rt = jnp.concatenate([w_router[li], jnp.zeros((d, LANES - N_EXPERTS), F32)], axis=1)
        b_rt = jnp.concatenate(
            [b_router[li], jnp.full((LANES - N_EXPERTS,), -jnp.inf, F32)]).reshape(1, LANES)

        h1, logits = _mixer(h, w_cat, conv_w[li], row(conv_b), w_ri, row(lru_b_r), row(lru_b_i),
                            row(lru_lambda), w_g, row(gla_b_gate), row(gla_norm_g),
                            w_out[li].astype(BF16), row(ln1_g), row(ln1_b), w_rt, b_rt)
        h1 = h1.reshape(t, d)

        eidx, gates, rank, counts = _router(logits.reshape(t, LANES))

        counts = counts[0, :N_EXPERTS]
        padded = (counts + BLK - 1) // BLK * BLK
        padded_end = jnp.cumsum(padded)
        padded_start = padded_end - padded
        e = eidx[:, :TOP_K]
        pos = (padded_start[e] + rank[:, :TOP_K]).reshape(-1).astype(jnp.int32)
        n_slots = t * TOP_K + N_EXPERTS * BLK
        n_blocks = n_slots // BLK
        token_flat = jnp.arange(t * TOP_K, dtype=jnp.int32) // TOP_K
        slot_token = jnp.zeros((n_slots,), jnp.int32).at[pos].set(token_flat)
        slot_gate = jnp.zeros((n_slots,), F32).at[pos].set(gates[:, :TOP_K].reshape(-1))
        block_expert = jnp.minimum(
            jnp.searchsorted(padded_end, jnp.arange(n_blocks, dtype=jnp.int32) * BLK, side='right'),
            N_EXPERTS - 1).astype(jnp.int32)
        n_used = (padded_end[-1:] // BLK).astype(jnp.int32)

        y_slots = _experts(block_expert, n_used, slot_token, h1, slot_gate.reshape(n_slots, 1),
                           w_up[li], b_up[li].reshape(N_EXPERTS, 1, 2 * D_FF),
                           w_down[li], b_down[li].reshape(N_EXPERTS, 1, d))

        out = _final(pos, y_slots, h1, p[li].reshape(t, PLE_DIM), row(ln2_g), row(ln2_b),
                     w_ple[li].astype(BF16), w_ple_gate[li].astype(BF16), row(b_ple_gate))
        h = out.reshape(bsz, seq, d)
    return h
```

```python
import functools

import jax
import jax.numpy as jnp
from jax import lax
from jax.experimental import pallas as pl
from jax.experimental.pallas import tpu as pltpu

F32 = jnp.float32
BF16 = jnp.bfloat16

D_MODEL = 1024
PLE_DIM = 256
LRU_BLOCKS = 8
LRU_BLOCK = 128
CONV_WIDTH = 4
LRU_C = 8.0
GLA_HEADS = 4
GLA_DK = 128
GLA_DV = 256
GLA_RANK = 16
GLA_TAU = 16.0
GLA_CHUNK = 64
N_EXPERTS = 32
TOP_K = 4
D_FF = 1024
SWIGLU_LIMIT = 7.0
SWIGLU_ALPHA = 1.702
LN_EPS = 1e-5
RMS_EPS = 1e-5
DEEPNORM_ALPHA = 2.0 ** 0.25

LANES = 128
SUBLANES = 8
VMEM_LIMIT = 56 * 1024 * 1024

QK = GLA_HEADS * GLA_DK
VW = GLA_HEADS * GLA_DV
OFF_XA = 0
OFF_GA = OFF_XA + D_MODEL
OFF_Q = OFF_GA + D_MODEL
OFF_K = OFF_Q + QK
OFF_V = OFF_K + QK
OFF_GO = OFF_V + VW
OFF_GLR = OFF_GO + VW
OFF_MA = OFF_GLR + LANES
OFF_MB = OFF_MA + D_MODEL
IN_PACKED = OFF_MB + D_MODEL

TS = 256
TR = 512
BLK = 256
TC = 256
INVERT_STEPS = 128


def _dot(a, b):
    return jnp.dot(a, b, preferred_element_type=F32)


def _softplus(x):
    return jnp.maximum(x, 0.0) + jnp.log1p(jnp.exp(-jnp.abs(x)))


def _layer_norm(x, g, b):
    mu = jnp.mean(x, axis=-1, keepdims=True)
    xc = x - mu
    var = jnp.mean(xc * xc, axis=-1, keepdims=True)
    return xc * lax.rsqrt(var + LN_EPS) * g + b


def _mixer_kernel(x_ref, win_ref, convw_ref, convb_ref, wri_ref, br_ref, bi_ref, lam_ref,
                  wg_ref, bg_ref, gnorm_ref, wout_ref, ln1g_ref, ln1b_ref, wrt_ref, brt_ref,
                  h1_ref, logits_ref,
                  xa_buf, a_buf, u_buf, y_buf, q_buf, k_buf, v_buf, lg_buf, o_buf, hc_ref, st_ref):
    s = pl.program_id(1)

    @pl.when(s == 0)
    def _():
        xa_buf[0:SUBLANES, :] = jnp.zeros((SUBLANES, D_MODEL), F32)
        hc_ref[...] = jnp.zeros_like(hc_ref)
        st_ref[...] = jnp.zeros_like(st_ref)

    xb = x_ref[...].astype(BF16)

    def proj(off, width):
        return _dot(xb, win_ref[:, off:off + width])

    xa_buf[SUBLANES:SUBLANES + TS, :] = proj(OFF_XA, D_MODEL)
    base = SUBLANES - (CONV_WIDTH - 1)
    xc = xa_buf[base:base + TS, :] * convw_ref[0:1, :]
    for kk in range(1, CONV_WIDTH):
        xc = xc + xa_buf[base + kk:base + kk + TS, :] * convw_ref[kk:kk + 1, :]
    xc = xc + convb_ref[...]
    xa_buf[0:SUBLANES, :] = xa_buf[TS:TS + SUBLANES, :]

    neg_c_sp = -LRU_C * _softplus(-lam_ref[...])
    for g in range(LRU_BLOCKS):
        sl = slice(g * LRU_BLOCK, (g + 1) * LRU_BLOCK)
        xg = xc[:, sl]
        ri = _dot(xg.astype(BF16), wri_ref[g])
        r = jax.nn.sigmoid(ri[:, :LRU_BLOCK] + br_ref[:, sl])
        i = jax.nn.sigmoid(ri[:, LRU_BLOCK:] + bi_ref[:, sl])
        log_a = neg_c_sp[:, sl] * r
        a = jnp.exp(log_a)
        a_buf[:, sl] = a
        u_buf[:, sl] = jnp.sqrt(jnp.tanh(-log_a) * (a * a + 1.0)) * (i * xg)

    def scan_body(t, h):
        h = a_buf[pl.ds(t, 1), :] * h + u_buf[pl.ds(t, 1), :]
        u_buf[pl.ds(t, 1), :] = h
        return h

    hc_ref[...] = lax.fori_loop(0, TS, scan_body, hc_ref[...], unroll=8)

    ga = proj(OFF_GA, D_MODEL)
    ma = proj(OFF_MA, D_MODEL)
    y_buf[...] = jax.nn.sigmoid(ma) * (u_buf[...] * jax.nn.gelu(ga))

    q_buf[...] = proj(OFF_Q, QK) * (GLA_DK ** -0.5)
    k_buf[...] = proj(OFF_K, QK)
    v_buf[...] = proj(OFF_V, VW)
    glr = proj(OFF_GLR, LANES)
    z = _dot(glr.astype(BF16), wg_ref[...]) + bg_ref[...]
    lg_buf[...] = (jnp.minimum(z, 0.0) - jnp.log1p(jnp.exp(-jnp.abs(z)))) * (1.0 / GLA_TAU)

    rows = lax.broadcasted_iota(jnp.int32, (GLA_CHUNK, GLA_CHUNK), 0)
    cols = lax.broadcasted_iota(jnp.int32, (GLA_CHUNK, GLA_CHUNK), 1)
    causal = rows >= cols
    tri = causal.astype(BF16)
    mid = GLA_CHUNK // 2 - 1

    def chunk_body(c, carry):
        r0 = pl.multiple_of(c * GLA_CHUNK, GLA_CHUNK)
        lg = lg_buf[pl.ds(r0, GLA_CHUNK), :]
        lg_hi = lg.astype(BF16)
        rem = lg - lg_hi.astype(F32)
        lg_mid = rem.astype(BF16)
        lg_lo = (rem - lg_mid.astype(F32)).astype(BF16)
        bcum = _dot(tri, lg_hi) + _dot(tri, lg_mid) + _dot(tri, lg_lo)
        for h in range(GLA_HEADS):
            ks = slice(h * GLA_DK, (h + 1) * GLA_DK)
            vs = slice(h * GLA_DV, (h + 1) * GLA_DV)
            b = bcum[:, ks]
            qh = q_buf[pl.ds(r0, GLA_CHUNK), ks]
            kh = k_buf[pl.ds(r0, GLA_CHUNK), ks]
            vh = v_buf[pl.ds(r0, GLA_CHUNK), vs].astype(BF16)
            b_last = b[GLA_CHUNK - 1:GLA_CHUNK, :]
            b_mid = b[mid:mid + 1, :]
            st = st_ref[h]
            q_in = (qh * jnp.exp(b)).astype(BF16)
            o_inter = lax.dot_general(q_in, st.astype(BF16), (((1,), (1,)), ((), ())),
                                      preferred_element_type=F32)
            q_t = (qh * jnp.exp(b - b_mid)).astype(BF16)
            k_t = (kh * jnp.exp(b_mid - b)).astype(BF16)
            scores = lax.dot_general(q_t, k_t, (((1,), (1,)), ((), ())), preferred_element_type=F32)
            scores = jnp.where(causal, scores, 0.0).astype(BF16)
            o = o_inter + _dot(scores, vh)
            k_dec = (kh * jnp.exp(b_last - b)).astype(BF16)
            st_ref[h] = st * jnp.exp(b_last) + lax.dot_general(
                vh, k_dec, (((0,), (0,)), ((), ())), preferred_element_type=F32)
            ms = jnp.mean(o * o, axis=-1, keepdims=True)
            o_buf[pl.ds(r0, GLA_CHUNK), vs] = o * lax.rsqrt(ms + RMS_EPS) * gnorm_ref[...]
        return carry

    lax.fori_loop(0, TS // GLA_CHUNK, chunk_body, 0)

    go = proj(OFF_GO, VW)
    mb = proj(OFF_MB, D_MODEL)
    y = y_buf[...] + jax.nn.sigmoid(mb) * (o_buf[...] * (go * jax.nn.sigmoid(go)))

    hpre = DEEPNORM_ALPHA * x_ref[...] + _dot(y.astype(BF16), wout_ref[...])
    h1 = _layer_norm(hpre, ln1g_ref[...], ln1b_ref[...])
    h1_ref[...] = h1
    logits_ref[...] = jnp.dot(h1, wrt_ref[...], preferred_element_type=F32,
                              precision=lax.Precision.HIGHEST) + brt_ref[...]


def _mixer(x, w_cat, conv_w, conv_b, w_ri, b_r, b_i, lam, w_g, b_g, gnorm, w_out, ln1_g, ln1_b, w_rt, b_rt):
    bsz, seq, d = x.shape
    const2 = lambda shape: pl.BlockSpec(shape, lambda b, s: (0, 0))
    const3 = lambda shape: pl.BlockSpec(shape, lambda b, s: (0, 0, 0))
    return pl.pallas_call(
        _mixer_kernel,
        grid=(bsz, seq // TS),
        in_specs=[
            pl.BlockSpec((None, TS, d), lambda b, s: (b, s, 0)),
            const2((d, IN_PACKED)),
            const2((CONV_WIDTH, d)), const2((1, d)),
            const3((LRU_BLOCKS, LRU_BLOCK, 2 * LRU_BLOCK)), const2((1, d)), const2((1, d)), const2((1, d)),
            const2((LANES, QK)), const2((1, QK)), const2((1, GLA_DV)),
            const2((d, d)), const2((1, d)), const2((1, d)),
            const2((d, LANES)), const2((1, LANES)),
        ],
        out_specs=[
            pl.BlockSpec((None, TS, d), lambda b, s: (b, s, 0)),
            pl.BlockSpec((None, TS, LANES), lambda b, s: (b, s, 0)),
        ],
        out_shape=[
            jax.ShapeDtypeStruct((bsz, seq, d), F32),
            jax.ShapeDtypeStruct((bsz, seq, LANES), F32),
        ],
        scratch_shapes=[
            pltpu.VMEM((TS + SUBLANES, d), F32),
            pltpu.VMEM((TS, d), F32),
            pltpu.VMEM((TS, d), F32),
            pltpu.VMEM((TS, d), F32),
            pltpu.VMEM((TS, QK), F32),
            pltpu.VMEM((TS, QK), F32),
            pltpu.VMEM((TS, VW), F32),
            pltpu.VMEM((TS, QK), F32),
            pltpu.VMEM((TS, VW), F32),
            pltpu.VMEM((1, d), F32),
            pltpu.VMEM((GLA_HEADS, GLA_DV, GLA_DK), F32),
        ],
        compiler_params=pltpu.CompilerParams(
            dimension_semantics=("arbitrary", "arbitrary"), vmem_limit_bytes=VMEM_LIMIT),
        name="mixer",
    )(x, w_cat, conv_w, conv_b, w_ri, b_r, b_i, lam, w_g, b_g, gnorm, w_out, ln1_g, ln1_b, w_rt, b_rt)


def _router_kernel(logits_ref, eidx_ref, gate_ref, rank_ref, cnt_ref, carry_ref):
    i = pl.program_id(0)

    @pl.when(i == 0)
    def _():
        carry_ref[...] = jnp.zeros_like(carry_ref)

    lane = lax.broadcasted_iota(jnp.int32, (TR, LANES), 1)
    work = logits_ref[...]
    vals, idxs, sels = [], [], []
    for _ in range(TOP_K):
        m = jnp.max(work, axis=-1, keepdims=True)
        idx = jnp.min(jnp.where(work == m, lane, LANES), axis=-1, keepdims=True)
        sel = lane == idx
        vals.append(m)
        idxs.append(idx)
        sels.append(sel)
        work = jnp.where(sel, -jnp.inf, work)

    es = [jnp.exp(v - vals[0]) for v in vals]
    denom = es[0] + es[1] + es[2] + es[3]

    member = jnp.zeros((TR, LANES), F32)
    for sel in sels:
        member = member + sel.astype(F32)
    rows = lax.broadcasted_iota(jnp.int32, (TR, TR), 0)
    cols = lax.broadcasted_iota(jnp.int32, (TR, TR), 1)
    strict = (rows > cols).astype(BF16)
    before = _dot(strict, member.astype(BF16)) + carry_ref[0:1, :]

    eidx = jnp.zeros((TR, LANES), jnp.int32)
    gate = jnp.zeros((TR, LANES), F32)
    rank = jnp.zeros((TR, LANES), jnp.int32)
    for kk in range(TOP_K):
        rk = jnp.sum(jnp.where(sels[kk], before, 0.0), axis=-1, keepdims=True).astype(jnp.int32)
        eidx = jnp.where(lane == kk, idxs[kk], eidx)
        gate = jnp.where(lane == kk, es[kk] / denom, gate)
        rank = jnp.where(lane == kk, rk, rank)
    eidx_ref[...] = eidx
    gate_ref[...] = gate
    rank_ref[...] = rank

    total = carry_ref[0:1, :] + jnp.sum(member, axis=0, keepdims=True)
    carry_ref[...] = jnp.broadcast_to(total, carry_ref.shape)
    cnt_ref[...] = jnp.broadcast_to(total, cnt_ref.shape).astype(jnp.int32)


def _router(logits):
    t = logits.shape[0]
    tile = pl.BlockSpec((TR, LANES), lambda i: (i, 0))
    return pl.pallas_call(
        _router_kernel,
        grid=(t // TR,),
        in_specs=[tile],
        out_specs=[tile, tile, tile, pl.BlockSpec((SUBLANES, LANES), lambda i: (0, 0))],
        out_shape=[
            jax.ShapeDtypeStruct((t, LANES), jnp.int32),
            jax.ShapeDtypeStruct((t, LANES), F32),
            jax.ShapeDtypeStruct((t, LANES), jnp.int32),
            jax.ShapeDtypeStruct((SUBLANES, LANES), jnp.int32),
        ],
        scratch_shapes=[pltpu.VMEM((SUBLANES, LANES), F32)],
        compiler_params=pltpu.CompilerParams(dimension_semantics=("arbitrary",)),
        name="router",
    )(logits)


def _invert_kernel(pos_ref, dest_ref):
    n_assign = pos_ref.shape[0]
    t = n_assign // TOP_K
    n_ext = dest_ref.shape[0]
    phase = pl.program_id(0)
    j = pl.program_id(1)
    fill_chunk = n_ext // INVERT_STEPS
    scat_chunk = n_assign // INVERT_STEPS

    @pl.when(phase == 0)
    def _():
        base = j * fill_chunk
        for r in range(fill_chunk):
            dest_ref[base + r] = n_assign + ((base + r) & (BLK - 1))

    @pl.when(phase == 1)
    def _():
        base = j * scat_chunk
        tok0 = j * (scat_chunk // TOP_K)
        for r in range(scat_chunk):
            dest_ref[BLK + pos_ref[base + r]] = (r % TOP_K) * t + tok0 + r // TOP_K


def _invert(pos, n_ext):
    assert n_ext % INVERT_STEPS == 0 and pos.shape[0] % (INVERT_STEPS * TOP_K) == 0
    return pl.pallas_call(
        _invert_kernel,
        grid=(2, INVERT_STEPS),
        in_specs=[pl.BlockSpec(memory_space=pltpu.SMEM)],
        out_specs=pl.BlockSpec(memory_space=pltpu.SMEM),
        out_shape=jax.ShapeDtypeStruct((n_ext,), jnp.int32),
        compiler_params=pltpu.CompilerParams(dimension_semantics=("arbitrary", "arbitrary")),
        name="invert",
    )(pos)


def _row_copy(src, src_row, dst, dst_row, sem):
    return pltpu.make_async_copy(src.at[pl.ds(src_row, 1), :], dst.at[pl.ds(dst_row, 1), :], sem)


def _expert_kernel(be_ref, nused_ref, dest_ref,
                   h1_hbm, wup_ref, bup_ref, wdn_ref, bdn_ref,
                   ycomb_hbm,
                   xbuf0, xbuf1, ybuf0, ybuf1, xb_tmp, gsem, ssem, wup_bf, wdn_bf):
    i = pl.program_id(0)
    n_used = nused_ref[0]
    n_blocks = be_ref.shape[0]
    t = h1_hbm.shape[0]
    xbufs = (xbuf0, xbuf1)
    ybufs = (ybuf0, ybuf1)

    def gather_rows(block, parity):
        base = (block + 1) * BLK
        for r in range(BLK):
            tok = dest_ref[base + r] & (t - 1)
            _row_copy(h1_hbm, tok, xbufs[parity], r, gsem.at[parity]).start()

    def scatter_rows(block, parity):
        base = (block + 1) * BLK
        for r in range(BLK):
            _row_copy(ybufs[parity], r, ycomb_hbm, dest_ref[base + r], ssem.at[parity]).start()

    def wait_gather(parity):
        pltpu.make_async_copy(h1_hbm.at[pl.ds(0, BLK), :], xbufs[parity], gsem.at[parity]).wait()

    def wait_scatter(parity):
        pltpu.make_async_copy(ybufs[parity], ycomb_hbm.at[pl.ds(0, BLK), :], ssem.at[parity]).wait()

    @pl.when(i == 0)
    def _():
        ybuf1[...] = jnp.zeros_like(ybuf1)
        gather_rows(0, 0)

    first_of_expert = jnp.logical_or(i == 0, be_ref[jnp.minimum(i, n_blocks - 1)] != be_ref[jnp.maximum(i - 1, 0)])

    @pl.when(jnp.logical_and(i < n_used, first_of_expert))
    def _():
        wup_bf[...] = wup_ref[...].astype(BF16)
        wdn_bf[...] = wdn_ref[...].astype(BF16)

    def step(parity):
        other = 1 - parity
        wait_gather(parity)

        @pl.when(i >= 1)
        def _():
            wait_scatter(parity)

        xb_tmp[...] = xbufs[parity][...].astype(BF16)
        gather_rows(jnp.minimum(i + 1, n_blocks - 1), other)
        scatter_rows(i - 1, other)
        hdn = _dot(xb_tmp[...], wup_bf[...]) + bup_ref[...]
        gate_h = jnp.minimum(hdn[:, :D_FF], SWIGLU_LIMIT)
        up_h = jnp.clip(hdn[:, D_FF:], -SWIGLU_LIMIT, SWIGLU_LIMIT)
        act = (up_h + 1.0) * gate_h * jax.nn.sigmoid(SWIGLU_ALPHA * gate_h)
        ybufs[parity][...] = _dot(act.astype(BF16), wdn_bf[...]) + bdn_ref[...]

    def drain(parity):
        other = 1 - parity
        wait_gather(parity)
        wait_scatter(parity)
        scatter_rows(i - 1, other)
        wait_scatter(other)

    for parity in range(2):
        @pl.when(jnp.logical_and(i < n_used, i % 2 == parity))
        def _():
            step(parity)

        @pl.when(jnp.logical_and(i == n_used, i % 2 == parity))
        def _():
            drain(parity)


def _experts(block_expert, n_used, dest_ext, h1, w_up, b_up, w_down, b_down):
    n_blocks = block_expert.shape[0]
    t, d = h1.shape
    last = n_blocks - 1
    wmap = lambda i, be, nu, de: (be[jnp.minimum(i, last)], 0, 0)
    grid_spec = pltpu.PrefetchScalarGridSpec(
        num_scalar_prefetch=3,
        grid=(n_blocks + 1,),
        in_specs=[
            pl.BlockSpec(memory_space=pl.ANY),
            pl.BlockSpec((None, d, 2 * D_FF), wmap),
            pl.BlockSpec((None, 1, 2 * D_FF), wmap),
            pl.BlockSpec((None, D_FF, d), wmap),
            pl.BlockSpec((None, 1, d), wmap),
        ],
        out_specs=pl.BlockSpec(memory_space=pl.ANY),
        scratch_shapes=[
            pltpu.VMEM((BLK, d), F32), pltpu.VMEM((BLK, d), F32),
            pltpu.VMEM((BLK, d), F32), pltpu.VMEM((BLK, d), F32),
            pltpu.VMEM((BLK, d), BF16),
            pltpu.SemaphoreType.DMA((2,)),
            pltpu.SemaphoreType.DMA((2,)),
            pltpu.VMEM((d, 2 * D_FF), BF16),
            pltpu.VMEM((D_FF, d), BF16),
        ],
    )
    return pl.pallas_call(
        _expert_kernel,
        grid_spec=grid_spec,
        out_shape=jax.ShapeDtypeStruct((TOP_K * t + BLK, d), F32),
        compiler_params=pltpu.CompilerParams(
            dimension_semantics=("arbitrary",), vmem_limit_bytes=VMEM_LIMIT),
        name="experts",
    )(block_expert, n_used, dest_ext, h1, w_up, b_up, w_down, b_down)


def _final_kernel(y0_ref, y1_ref, y2_ref, y3_ref, gate_ref, h1_ref, p_ref, ln2g_ref, ln2b_ref,
                  wple_ref, wpg_ref, bpg_ref, out_ref):
    g = gate_ref[...]
    m = y0_ref[...] * g[:, 0:1]
    for kk, y_ref in enumerate((y1_ref, y2_ref, y3_ref), start=1):
        m = m + y_ref[...] * g[:, kk:kk + 1]
    h2 = _layer_norm(DEEPNORM_ALPHA * h1_ref[...] + m, ln2g_ref[...], ln2b_ref[...])
    gate = jax.nn.sigmoid(_dot(h2.astype(BF16), wpg_ref[...]) + bpg_ref[...])
    out_ref[...] = h2 + gate * _dot(p_ref[...].astype(BF16), wple_ref[...])


def _final(ycomb, gates, h1, p, ln2_g, ln2_b, w_ple, w_pg, b_pg):
    t, d = h1.shape
    n_tiles = t // TC
    const = lambda shape: pl.BlockSpec(shape, lambda i: (0, 0))
    ycomb_k = lambda kk: pl.BlockSpec((TC, d), lambda i: (kk * n_tiles + i, 0))
    return pl.pallas_call(
        _final_kernel,
        grid=(n_tiles,),
        in_specs=[
            ycomb_k(0), ycomb_k(1), ycomb_k(2), ycomb_k(3),
            pl.BlockSpec((TC, LANES), lambda i: (i, 0)),
            pl.BlockSpec((TC, d), lambda i: (i, 0)),
            pl.BlockSpec((TC, PLE_DIM), lambda i: (i, 0)),
            const((1, d)), const((1, d)),
            const((PLE_DIM, d)), const((d, d)), const((1, d)),
        ],
        out_specs=pl.BlockSpec((TC, d), lambda i: (i, 0)),
        out_shape=jax.ShapeDtypeStruct((t, d), F32),
        compiler_params=pltpu.CompilerParams(
            dimension_semantics=("arbitrary",), vmem_limit_bytes=VMEM_LIMIT),
        name="final",
    )(ycomb, ycomb, ycomb, ycomb, gates, h1, p, ln2_g, ln2_b, w_ple, w_pg, b_pg)


def kernel(x, p, w_in, conv_w, conv_b, lru_w_r, lru_b_r, lru_w_i, lru_b_i, lru_lambda, gla_w_gate, gla_b_gate, gla_norm_g, w_out, ln1_g, ln1_b, w_router, b_router, w_up, b_up, w_down, b_down, ln2_g, ln2_b, w_ple, w_ple_gate, b_ple_gate):
    bsz, seq, d = x.shape
    t = bsz * seq
    h = x
    for li in range(w_in.shape[0]):
        row = lambda v: v[li].reshape(1, -1)
        w = w_in[li]
        glr_end = OFF_GLR + GLA_RANK
        w_cat = jnp.concatenate(
            [w[:, :glr_end], jnp.zeros((d, LANES - GLA_RANK), w.dtype), w[:, glr_end:]],
            axis=1).astype(BF16)
        w_ri = jnp.concatenate([lru_w_r[li], lru_w_i[li]], axis=-1).astype(BF16)
        w_g = jnp.concatenate(
            [gla_w_gate[li], jnp.zeros((LANES - GLA_RANK, QK), F32)], axis=0).astype(BF16)
        w_rt = jnp.concatenate([w_router[li], jnp.zeros((d, LANES - N_EXPERTS), F32)], axis=1)
        b_rt = jnp.concatenate(
            [b_router[li], jnp.full((LANES - N_EXPERTS,), -jnp.inf, F32)]).reshape(1, LANES)

        h1, logits = _mixer(h, w_cat, conv_w[li], row(conv_b), w_ri, row(lru_b_r), row(lru_b_i),
                            row(lru_lambda), w_g, row(gla_b_gate), row(gla_norm_g),
                            w_out[li].astype(BF16), row(ln1_g), row(ln1_b), w_rt, b_rt)
        h1 = h1.reshape(t, d)

        eidx, gates, rank, counts = _router(logits.reshape(t, LANES))

        counts = counts[0, :N_EXPERTS]
        padded = (counts + BLK - 1) // BLK * BLK
        padded_end = jnp.cumsum(padded)
        padded_start = padded_end - padded
        e = eidx[:, :TOP_K]
        onehot = e[:, :, None] == jnp.arange(N_EXPERTS, dtype=jnp.int32)
        pos = jnp.sum(jnp.where(onehot, padded_start, 0), axis=-1) + rank[:, :TOP_K]
        pos = pos.reshape(-1).astype(jnp.int32)
        n_slots = t * TOP_K + N_EXPERTS * BLK
        n_blocks = n_slots // BLK
        block_row0 = jnp.arange(n_blocks, dtype=jnp.int32)[:, None] * BLK
        block_expert = jnp.minimum(
            jnp.sum((padded_end[None, :] <= block_row0).astype(jnp.int32), axis=-1), N_EXPERTS - 1)
        n_used = (padded_end[-1:] // BLK).astype(jnp.int32)

        dest_ext = _invert(pos, (n_blocks + 1) * BLK)
        ycomb = _experts(block_expert, n_used, dest_ext, h1,
                         w_up[li], b_up[li].reshape(N_EXPERTS, 1, 2 * D_FF),
                         w_down[li], b_down[li].reshape(N_EXPERTS, 1, d))

        out = _final(ycomb, gates, h1, p[li].reshape(t, PLE_DIM), row(ln2_g), row(ln2_b),
                     w_ple[li].astype(BF16), w_ple_gate[li].astype(BF16), row(b_ple_gate))
        h = out.reshape(bsz, seq, d)
    return h
```

```python
import functools

import jax
import jax.numpy as jnp
from jax import lax
from jax.experimental import pallas as pl
from jax.experimental.pallas import tpu as pltpu

F32 = jnp.float32
BF16 = jnp.bfloat16

D_MODEL = 1024
PLE_DIM = 256
LRU_BLOCKS = 8
LRU_BLOCK = 128
CONV_WIDTH = 4
LRU_C = 8.0
GLA_HEADS = 4
GLA_DK = 128
GLA_DV = 256
GLA_RANK = 16
GLA_TAU = 16.0
GLA_CHUNK = 64
N_EXPERTS = 32
TOP_K = 4
D_FF = 1024
SWIGLU_LIMIT = 7.0
SWIGLU_ALPHA = 1.702
LN_EPS = 1e-5
RMS_EPS = 1e-5
DEEPNORM_ALPHA = 2.0 ** 0.25

LANES = 128
SUBLANES = 8
ROW_TILES = D_MODEL // LANES
assert ROW_TILES == SUBLANES
VMEM_LIMIT = 56 * 1024 * 1024

QK = GLA_HEADS * GLA_DK
VW = GLA_HEADS * GLA_DV
OFF_XA = 0
OFF_GA = OFF_XA + D_MODEL
OFF_Q = OFF_GA + D_MODEL
OFF_K = OFF_Q + QK
OFF_V = OFF_K + QK
OFF_GO = OFF_V + VW
OFF_GLR = OFF_GO + VW
OFF_MA = OFF_GLR + LANES
OFF_MB = OFF_MA + D_MODEL
IN_PACKED = OFF_MB + D_MODEL

TS = 256
TR = 512
BLK = 256
TC = 256
INVERT_STEPS = 128


def _dot(a, b):
    return jnp.dot(a, b, preferred_element_type=F32)


def _softplus(x):
    return jnp.maximum(x, 0.0) + jnp.log1p(jnp.exp(-jnp.abs(x)))


def _layer_norm(x, g, b):
    mu = jnp.mean(x, axis=-1, keepdims=True)
    xc = x - mu
    var = jnp.mean(xc * xc, axis=-1, keepdims=True)
    return xc * lax.rsqrt(var + LN_EPS) * g + b


def _mixer_kernel(x_ref, win_ref, convw_ref, convb_ref, wri_ref, br_ref, bi_ref, lam_ref,
                  wg_ref, bg_ref, gnorm_ref, wout_ref, ln1g_ref, ln1b_ref, wrt_ref, brt_ref,
                  h1_ref, logits_ref,
                  xa_buf, a_buf, u_buf, y_buf, q_buf, k_buf, v_buf, lg_buf, o_buf, hc_ref, st_ref):
    s = pl.program_id(1)

    @pl.when(s == 0)
    def _():
        xa_buf[0:SUBLANES, :] = jnp.zeros((SUBLANES, D_MODEL), F32)
        hc_ref[...] = jnp.zeros_like(hc_ref)
        st_ref[...] = jnp.zeros_like(st_ref)

    xb = x_ref[...].astype(BF16)

    def proj(off, width):
        return _dot(xb, win_ref[:, off:off + width])

    xa_buf[SUBLANES:SUBLANES + TS, :] = proj(OFF_XA, D_MODEL)
    base = SUBLANES - (CONV_WIDTH - 1)
    xc = xa_buf[base:base + TS, :] * convw_ref[0:1, :]
    for kk in range(1, CONV_WIDTH):
        xc = xc + xa_buf[base + kk:base + kk + TS, :] * convw_ref[kk:kk + 1, :]
    xc = xc + convb_ref[...]
    xa_buf[0:SUBLANES, :] = xa_buf[TS:TS + SUBLANES, :]

    neg_c_sp = -LRU_C * _softplus(-lam_ref[...])
    for g in range(LRU_BLOCKS):
        sl = slice(g * LRU_BLOCK, (g + 1) * LRU_BLOCK)
        xg = xc[:, sl]
        ri = _dot(xg.astype(BF16), wri_ref[g])
        r = jax.nn.sigmoid(ri[:, :LRU_BLOCK] + br_ref[:, sl])
        i = jax.nn.sigmoid(ri[:, LRU_BLOCK:] + bi_ref[:, sl])
        log_a = neg_c_sp[:, sl] * r
        a = jnp.exp(log_a)
        a_buf[:, sl] = a
        u_buf[:, sl] = jnp.sqrt(jnp.tanh(-log_a) * (a * a + 1.0)) * (i * xg)

    def scan_body(t, h):
        h = a_buf[pl.ds(t, 1), :] * h + u_buf[pl.ds(t, 1), :]
        u_buf[pl.ds(t, 1), :] = h
        return h

    hc_ref[...] = lax.fori_loop(0, TS, scan_body, hc_ref[...], unroll=8)

    ga = proj(OFF_GA, D_MODEL)
    ma = proj(OFF_MA, D_MODEL)
    y_buf[...] = jax.nn.sigmoid(ma) * (u_buf[...] * jax.nn.gelu(ga))

    q_buf[...] = proj(OFF_Q, QK) * (GLA_DK ** -0.5)
    k_buf[...] = proj(OFF_K, QK)
    v_buf[...] = proj(OFF_V, VW)
    glr = proj(OFF_GLR, LANES)
    z = _dot(glr.astype(BF16), wg_ref[...]) + bg_ref[...]
    lg_buf[...] = (jnp.minimum(z, 0.0) - jnp.log1p(jnp.exp(-jnp.abs(z)))) * (1.0 / GLA_TAU)

    rows = lax.broadcasted_iota(jnp.int32, (GLA_CHUNK, GLA_CHUNK), 0)
    cols = lax.broadcasted_iota(jnp.int32, (GLA_CHUNK, GLA_CHUNK), 1)
    causal = rows >= cols
    tri = causal.astype(BF16)
    mid = GLA_CHUNK // 2 - 1

    def chunk_body(c, carry):
        r0 = pl.multiple_of(c * GLA_CHUNK, GLA_CHUNK)
        lg = lg_buf[pl.ds(r0, GLA_CHUNK), :]
        lg_hi = lg.astype(BF16)
        rem = lg - lg_hi.astype(F32)
        lg_mid = rem.astype(BF16)
        lg_lo = (rem - lg_mid.astype(F32)).astype(BF16)
        bcum = _dot(tri, lg_hi) + _dot(tri, lg_mid) + _dot(tri, lg_lo)
        for h in range(GLA_HEADS):
            ks = slice(h * GLA_DK, (h + 1) * GLA_DK)
            vs = slice(h * GLA_DV, (h + 1) * GLA_DV)
            b = bcum[:, ks]
            qh = q_buf[pl.ds(r0, GLA_CHUNK), ks]
            kh = k_buf[pl.ds(r0, GLA_CHUNK), ks]
            vh = v_buf[pl.ds(r0, GLA_CHUNK), vs].astype(BF16)
            b_last = b[GLA_CHUNK - 1:GLA_CHUNK, :]
            b_mid = b[mid:mid + 1, :]
            st = st_ref[h]
            q_in = (qh * jnp.exp(b)).astype(BF16)
            o_inter = lax.dot_general(q_in, st.astype(BF16), (((1,), (1,)), ((), ())),
                                      preferred_element_type=F32)
            q_t = (qh * jnp.exp(b - b_mid)).astype(BF16)
            k_t = (kh * jnp.exp(b_mid - b)).astype(BF16)
            scores = lax.dot_general(q_t, k_t, (((1,), (1,)), ((), ())), preferred_element_type=F32)
            scores = jnp.where(causal, scores, 0.0).astype(BF16)
            o = o_inter + _dot(scores, vh)
            k_dec = (kh * jnp.exp(b_last - b)).astype(BF16)
            st_ref[h] = st * jnp.exp(b_last) + lax.dot_general(
                vh, k_dec, (((0,), (0,)), ((), ())), preferred_element_type=F32)
            ms = jnp.mean(o * o, axis=-1, keepdims=True)
            o_buf[pl.ds(r0, GLA_CHUNK), vs] = o * lax.rsqrt(ms + RMS_EPS) * gnorm_ref[...]
        return carry

    lax.fori_loop(0, TS // GLA_CHUNK, chunk_body, 0)

    go = proj(OFF_GO, VW)
    mb = proj(OFF_MB, D_MODEL)
    y = y_buf[...] + jax.nn.sigmoid(mb) * (o_buf[...] * (go * jax.nn.sigmoid(go)))

    hpre = DEEPNORM_ALPHA * x_ref[...] + _dot(y.astype(BF16), wout_ref[...])
    h1 = _layer_norm(hpre, ln1g_ref[...], ln1b_ref[...])
    h1_ref[...] = h1
    logits_ref[...] = jnp.dot(h1, wrt_ref[...], preferred_element_type=F32,
                              precision=lax.Precision.HIGHEST) + brt_ref[...]


def _mixer(x, w_cat, conv_w, conv_b, w_ri, b_r, b_i, lam, w_g, b_g, gnorm, w_out, ln1_g, ln1_b, w_rt, b_rt):
    bsz, seq, d = x.shape
    const2 = lambda shape: pl.BlockSpec(shape, lambda b, s: (0, 0))
    const3 = lambda shape: pl.BlockSpec(shape, lambda b, s: (0, 0, 0))
    return pl.pallas_call(
        _mixer_kernel,
        grid=(bsz, seq // TS),
        in_specs=[
            pl.BlockSpec((None, TS, d), lambda b, s: (b, s, 0)),
            const2((d, IN_PACKED)),
            const2((CONV_WIDTH, d)), const2((1, d)),
            const3((LRU_BLOCKS, LRU_BLOCK, 2 * LRU_BLOCK)), const2((1, d)), const2((1, d)), const2((1, d)),
            const2((LANES, QK)), const2((1, QK)), const2((1, GLA_DV)),
            const2((d, d)), const2((1, d)), const2((1, d)),
            const2((d, LANES)), const2((1, LANES)),
        ],
        out_specs=[
            pl.BlockSpec((None, TS, d), lambda b, s: (b, s, 0)),
            pl.BlockSpec((None, TS, LANES), lambda b, s: (b, s, 0)),
        ],
        out_shape=[
            jax.ShapeDtypeStruct((bsz, seq, d), F32),
            jax.ShapeDtypeStruct((bsz, seq, LANES), F32),
        ],
        scratch_shapes=[
            pltpu.VMEM((TS + SUBLANES, d), F32),
            pltpu.VMEM((TS, d), F32),
            pltpu.VMEM((TS, d), F32),
            pltpu.VMEM((TS, d), F32),
            pltpu.VMEM((TS, QK), F32),
            pltpu.VMEM((TS, QK), F32),
            pltpu.VMEM((TS, VW), F32),
            pltpu.VMEM((TS, QK), F32),
            pltpu.VMEM((TS, VW), F32),
            pltpu.VMEM((1, d), F32),
            pltpu.VMEM((GLA_HEADS, GLA_DV, GLA_DK), F32),
        ],
        compiler_params=pltpu.CompilerParams(
            dimension_semantics=("arbitrary", "arbitrary"), vmem_limit_bytes=VMEM_LIMIT),
        name="mixer",
    )(x, w_cat, conv_w, conv_b, w_ri, b_r, b_i, lam, w_g, b_g, gnorm, w_out, ln1_g, ln1_b, w_rt, b_rt)


def _router_kernel(logits_ref, eidx_ref, gate_ref, rank_ref, cnt_ref, carry_ref):
    i = pl.program_id(0)

    @pl.when(i == 0)
    def _():
        carry_ref[...] = jnp.zeros_like(carry_ref)

    lane = lax.broadcasted_iota(jnp.int32, (TR, LANES), 1)
    work = logits_ref[...]
    vals, idxs, sels = [], [], []
    for _ in range(TOP_K):
        m = jnp.max(work, axis=-1, keepdims=True)
        idx = jnp.min(jnp.where(work == m, lane, LANES), axis=-1, keepdims=True)
        sel = lane == idx
        vals.append(m)
        idxs.append(idx)
        sels.append(sel)
        work = jnp.where(sel, -jnp.inf, work)

    es = [jnp.exp(v - vals[0]) for v in vals]
    denom = es[0] + es[1] + es[2] + es[3]

    member = jnp.zeros((TR, LANES), F32)
    for sel in sels:
        member = member + sel.astype(F32)
    rows = lax.broadcasted_iota(jnp.int32, (TR, TR), 0)
    cols = lax.broadcasted_iota(jnp.int32, (TR, TR), 1)
    strict = (rows > cols).astype(BF16)
    before = _dot(strict, member.astype(BF16)) + carry_ref[0:1, :]

    eidx = jnp.zeros((TR, LANES), jnp.int32)
    gate = jnp.zeros((TR, LANES), F32)
    rank = jnp.zeros((TR, LANES), jnp.int32)
    for kk in range(TOP_K):
        rk = jnp.sum(jnp.where(sels[kk], before, 0.0), axis=-1, keepdims=True).astype(jnp.int32)
        eidx = jnp.where(lane == kk, idxs[kk], eidx)
        gate = jnp.where(lane == kk, es[kk] / denom, gate)
        rank = jnp.where(lane == kk, rk, rank)
    eidx_ref[...] = eidx
    gate_ref[...] = gate
    rank_ref[...] = rank

    total = carry_ref[0:1, :] + jnp.sum(member, axis=0, keepdims=True)
    carry_ref[...] = jnp.broadcast_to(total, carry_ref.shape)
    cnt_ref[...] = jnp.broadcast_to(total, cnt_ref.shape).astype(jnp.int32)


def _router(logits):
    t = logits.shape[0]
    tile = pl.BlockSpec((TR, LANES), lambda i: (i, 0))
    return pl.pallas_call(
        _router_kernel,
        grid=(t // TR,),
        in_specs=[tile],
        out_specs=[tile, tile, tile, pl.BlockSpec((SUBLANES, LANES), lambda i: (0, 0))],
        out_shape=[
            jax.ShapeDtypeStruct((t, LANES), jnp.int32),
            jax.ShapeDtypeStruct((t, LANES), F32),
            jax.ShapeDtypeStruct((t, LANES), jnp.int32),
            jax.ShapeDtypeStruct((SUBLANES, LANES), jnp.int32),
        ],
        scratch_shapes=[pltpu.VMEM((SUBLANES, LANES), F32)],
        compiler_params=pltpu.CompilerParams(dimension_semantics=("arbitrary",)),
        name="router",
    )(logits)


def _invert_kernel(pos_ref, dest_ref):
    n_assign = pos_ref.shape[0]
    t = n_assign // TOP_K
    n_ext = dest_ref.shape[0]
    phase = pl.program_id(0)
    j = pl.program_id(1)
    fill_chunk = n_ext // INVERT_STEPS
    scat_chunk = n_assign // INVERT_STEPS

    @pl.when(phase == 0)
    def _():
        base = j * fill_chunk
        for r in range(fill_chunk):
            dest_ref[base + r] = n_assign + ((base + r) & (BLK - 1))

    @pl.when(phase == 1)
    def _():
        base = j * scat_chunk
        tok0 = j * (scat_chunk // TOP_K)
        for r in range(scat_chunk):
            dest_ref[BLK + pos_ref[base + r]] = (r % TOP_K) * t + tok0 + r // TOP_K


def _invert(pos, n_ext):
    assert n_ext % INVERT_STEPS == 0 and pos.shape[0] % (INVERT_STEPS * TOP_K) == 0
    return pl.pallas_call(
        _invert_kernel,
        grid=(2, INVERT_STEPS),
        in_specs=[pl.BlockSpec(memory_space=pltpu.SMEM)],
        out_specs=pl.BlockSpec(memory_space=pltpu.SMEM),
        out_shape=jax.ShapeDtypeStruct((n_ext,), jnp.int32),
        compiler_params=pltpu.CompilerParams(dimension_semantics=("arbitrary", "arbitrary")),
        name="invert",
    )(pos)


def _row_copy(src, src_row, dst, dst_row, sem):
    return pltpu.make_async_copy(src.at[pl.ds(src_row * ROW_TILES, ROW_TILES), :],
                                 dst.at[pl.ds(dst_row * ROW_TILES, ROW_TILES), :], sem)


def _expert_kernel(be_ref, nused_ref, dest_ref,
                   h1_hbm, wup_ref, bup_ref, wdn_ref, bdn_ref,
                   ycomb_hbm,
                   xbuf0, xbuf1, ybuf0, ybuf1, xb_tmp, gsem, ssem, wup_bf, wdn_bf):
    i = pl.program_id(0)
    n_used = nused_ref[0]
    n_blocks = be_ref.shape[0]
    t = h1_hbm.shape[0] // ROW_TILES
    xbufs = (xbuf0, xbuf1)
    ybufs = (ybuf0, ybuf1)

    def gather_rows(block, parity):
        base = (block + 1) * BLK
        for r in range(BLK):
            tok = dest_ref[base + r] & (t - 1)
            _row_copy(h1_hbm, tok, xbufs[parity], r, gsem.at[parity]).start()

    def scatter_rows(block, parity):
        base = (block + 1) * BLK
        for r in range(BLK):
            _row_copy(ybufs[parity], r, ycomb_hbm, dest_ref[base + r], ssem.at[parity]).start()

    def wait_gather(parity):
        pltpu.make_async_copy(h1_hbm.at[pl.ds(0, BLK * ROW_TILES), :], xbufs[parity], gsem.at[parity]).wait()

    def wait_scatter(parity):
        pltpu.make_async_copy(ybufs[parity], ycomb_hbm.at[pl.ds(0, BLK * ROW_TILES), :], ssem.at[parity]).wait()

    @pl.when(i == 0)
    def _():
        ybuf1[...] = jnp.zeros_like(ybuf1)
        gather_rows(0, 0)

    first_of_expert = jnp.logical_or(i == 0, be_ref[jnp.minimum(i, n_blocks - 1)] != be_ref[jnp.maximum(i - 1, 0)])

    @pl.when(jnp.logical_and(i < n_used, first_of_expert))
    def _():
        wup_bf[...] = wup_ref[...].astype(BF16)
        wdn_bf[...] = wdn_ref[...].astype(BF16)

    def step(parity):
        other = 1 - parity
        wait_gather(parity)

        @pl.when(i >= 1)
        def _():
            wait_scatter(parity)

        for j in range(ROW_TILES):
            xb_tmp[:, j * LANES:(j + 1) * LANES] = xbufs[parity][pl.ds(j, BLK, stride=ROW_TILES), :].astype(BF16)
        gather_rows(jnp.minimum(i + 1, n_blocks - 1), other)
        scatter_rows(i - 1, other)
        hdn = _dot(xb_tmp[...], wup_bf[...]) + bup_ref[...]
        gate_h = jnp.minimum(hdn[:, :D_FF], SWIGLU_LIMIT)
        up_h = jnp.clip(hdn[:, D_FF:], -SWIGLU_LIMIT, SWIGLU_LIMIT)
        act = (up_h + 1.0) * gate_h * jax.nn.sigmoid(SWIGLU_ALPHA * gate_h)
        y = _dot(act.astype(BF16), wdn_bf[...]) + bdn_ref[...]
        for j in range(ROW_TILES):
            ybufs[parity][pl.ds(j, BLK, stride=ROW_TILES), :] = y[:, j * LANES:(j + 1) * LANES]

    def drain(parity):
        other = 1 - parity
        wait_gather(parity)
        wait_scatter(parity)
        scatter_rows(i - 1, other)
        wait_scatter(other)

    for parity in range(2):
        @pl.when(jnp.logical_and(i < n_used, i % 2 == parity))
        def _():
            step(parity)

        @pl.when(jnp.logical_and(i == n_used, i % 2 == parity))
        def _():
            drain(parity)


def _experts(block_expert, n_used, dest_ext, h1_tiles, w_up, b_up, w_down, b_down):
    n_blocks = block_expert.shape[0]
    t = h1_tiles.shape[0] // ROW_TILES
    d = D_MODEL
    last = n_blocks - 1
    wmap = lambda i, be, nu, de: (be[jnp.minimum(i, last)], 0, 0)
    grid_spec = pltpu.PrefetchScalarGridSpec(
        num_scalar_prefetch=3,
        grid=(n_blocks + 1,),
        in_specs=[
            pl.BlockSpec(memory_space=pl.ANY),
            pl.BlockSpec((None, d, 2 * D_FF), wmap),
            pl.BlockSpec((None, 1, 2 * D_FF), wmap),
            pl.BlockSpec((None, D_FF, d), wmap),
            pl.BlockSpec((None, 1, d), wmap),
        ],
        out_specs=pl.BlockSpec(memory_space=pl.ANY),
        scratch_shapes=[
            pltpu.VMEM((BLK * ROW_TILES, LANES), F32), pltpu.VMEM((BLK * ROW_TILES, LANES), F32),
            pltpu.VMEM((BLK * ROW_TILES, LANES), F32), pltpu.VMEM((BLK * ROW_TILES, LANES), F32),
            pltpu.VMEM((BLK, d), BF16),
            pltpu.SemaphoreType.DMA((2,)),
            pltpu.SemaphoreType.DMA((2,)),
            pltpu.VMEM((d, 2 * D_FF), BF16),
            pltpu.VMEM((D_FF, d), BF16),
        ],
    )
    return pl.pallas_call(
        _expert_kernel,
        grid_spec=grid_spec,
        out_shape=jax.ShapeDtypeStruct(((TOP_K * t + BLK) * ROW_TILES, LANES), F32),
        compiler_params=pltpu.CompilerParams(
            dimension_semantics=("arbitrary",), vmem_limit_bytes=VMEM_LIMIT),
        name="experts",
    )(block_expert, n_used, dest_ext, h1_tiles, w_up, b_up, w_down, b_down)


def _final_kernel(y0_ref, y1_ref, y2_ref, y3_ref, gate_ref, h1_ref, p_ref, ln2g_ref, ln2b_ref,
                  wple_ref, wpg_ref, bpg_ref, out_ref):
    g = gate_ref[...]
    parts = []
    for j in range(ROW_TILES):
        mj = y0_ref[pl.ds(j, TC, stride=ROW_TILES), :] * g[:, 0:1]
        for kk, y_ref in enumerate((y1_ref, y2_ref, y3_ref), start=1):
            mj = mj + y_ref[pl.ds(j, TC, stride=ROW_TILES), :] * g[:, kk:kk + 1]
        parts.append(mj)
    m = jnp.concatenate(parts, axis=1)
    h2 =_layer_norm(DEEPNORM_ALPHA * h1_ref[...] + m, ln2g_ref[...], ln2b_ref[...])
    gate = jax.nn.sigmoid(_dot(h2.astype(BF16), wpg_ref[...]) + bpg_ref[...])
    out_ref[...] = h2 + gate * _dot(p_ref[...].astype(BF16), wple_ref[...])


def _final(ycomb, gates, h1, p, ln2_g, ln2_b, w_ple, w_pg, b_pg):
    t, d = h1.shape
    n_tiles = t // TC
    const = lambda shape: pl.BlockSpec(shape, lambda i: (0, 0))
    ycomb_k = lambda kk: pl.BlockSpec((TC * ROW_TILES, LANES), lambda i: (kk * n_tiles + i, 0))
    return pl.pallas_call(
        _final_kernel,
        grid=(n_tiles,),
        in_specs=[
            ycomb_k(0), ycomb_k(1), ycomb_k(2), ycomb_k(3),
            pl.BlockSpec((TC, LANES), lambda i: (i, 0)),
            pl.BlockSpec((TC, d), lambda i: (i, 0)),
            pl.BlockSpec((TC, PLE_DIM), lambda i: (i, 0)),
            const((1, d)), const((1, d)),
            const((PLE_DIM, d)), const((d, d)), const((1, d)),
        ],
        out_specs=pl.BlockSpec((TC, d), lambda i: (i, 0)),
        out_shape=jax.ShapeDtypeStruct((t, d), F32),
        compiler_params=pltpu.CompilerParams(
            dimension_semantics=("arbitrary",), vmem_limit_bytes=VMEM_LIMIT),
        name="final",
    )(ycomb, ycomb, ycomb, ycomb, gates, h1, p, ln2_g, ln2_b, w_ple, w_pg, b_pg)


def kernel(x, p, w_in, conv_w, conv_b, lru_w_r, lru_b_r, lru_w_i, lru_b_i, lru_lambda, gla_w_gate, gla_b_gate, gla_norm_g, w_out, ln1_g, ln1_b, w_router, b_router, w_up, b_up, w_down, b_down, ln2_g, ln2_b, w_ple, w_ple_gate, b_ple_gate):
    bsz, seq, d = x.shape
    t = bsz * seq
    h = x
    for li in range(w_in.shape[0]):
        row = lambda v: v[li].reshape(1, -1)
        w = w_in[li]
        glr_end = OFF_GLR + GLA_RANK
        w_cat = jnp.concatenate(
            [w[:, :glr_end], jnp.zeros((d, LANES - GLA_RANK), w.dtype), w[:, glr_end:]],
            axis=1).astype(BF16)
        w_ri = jnp.concatenate([lru_w_r[li], lru_w_i[li]], axis=-1).astype(BF16)
        w_g = jnp.concatenate(
            [gla_w_gate[li], jnp.zeros((LANES - GLA_RANK, QK), F32)], axis=0).astype(BF16)
        w_rt = jnp.concatenate([w_router[li], jnp.zeros((d, LANES - N_EXPERTS), F32)], axis=1)
        b_rt = jnp.concatenate(
            [b_router[li], jnp.full((LANES - N_EXPERTS,), -jnp.inf, F32)]).reshape(1, LANES)

        h1, logits = _mixer(h, w_cat, conv_w[li], row(conv_b), w_ri, row(lru_b_r), row(lru_b_i),
                            row(lru_lambda), w_g, row(gla_b_gate), row(gla_norm_g),
                            w_out[li].astype(BF16), row(ln1_g), row(ln1_b), w_rt, b_rt)
        h1 = h1.reshape(t, d)

        eidx, gates, rank, counts = _router(logits.reshape(t, LANES))

        counts = counts[0, :N_EXPERTS]
        padded = (counts + BLK - 1) // BLK * BLK
        padded_end = jnp.cumsum(padded)
        padded_start = padded_end - padded
        e = eidx[:, :TOP_K]
        onehot = e[:, :, None] == jnp.arange(N_EXPERTS, dtype=jnp.int32)
        pos = jnp.sum(jnp.where(onehot, padded_start, 0), axis=-1) + rank[:, :TOP_K]
        pos = pos.reshape(-1).astype(jnp.int32)
        n_slots = t * TOP_K + N_EXPERTS * BLK
        n_blocks = n_slots // BLK
        block_row0 = jnp.arange(n_blocks, dtype=jnp.int32)[:, None] * BLK
        block_expert = jnp.minimum(
            jnp.sum((padded_end[None, :] <= block_row0).astype(jnp.int32), axis=-1), N_EXPERTS - 1)
        n_used = (padded_end[-1:] // BLK).astype(jnp.int32)

        dest_ext = _invert(pos, (n_blocks + 1) * BLK)
        ycomb = _experts(block_expert, n_used, dest_ext, h1.reshape(t * ROW_TILES, LANES),
                         w_up[li], b_up[li].reshape(N_EXPERTS, 1, 2 * D_FF),
                         w_down[li], b_down[li].reshape(N_EXPERTS, 1, d))

        out = _final(ycomb, gates, h1, p[li].reshape(t, PLE_DIM), row(ln2_g), row(ln2_b),
                     w_ple[li].astype(BF16), w_ple_gate[li].astype(BF16), row(b_ple_gate))
        h = out.reshape(bsz, seq, d)
    return h
```

```python
import functools

import jax
import jax.numpy as jnp
from jax import lax
from jax.experimental import pallas as pl
from jax.experimental.pallas import tpu as pltpu

F32 = jnp.float32
BF16 = jnp.bfloat16

D_MODEL = 1024
PLE_DIM = 256
LRU_BLOCKS = 8
LRU_BLOCK = 128
CONV_WIDTH = 4
LRU_C = 8.0
GLA_HEADS = 4
GLA_DK = 128
GLA_DV = 256
GLA_RANK = 16
GLA_TAU = 16.0
GLA_CHUNK = 64
N_EXPERTS = 32
TOP_K = 4
D_FF = 1024
SWIGLU_LIMIT = 7.0
SWIGLU_ALPHA = 1.702
LN_EPS = 1e-5
RMS_EPS = 1e-5
DEEPNORM_ALPHA = 2.0 ** 0.25

LANES = 128
SUBLANES = 8
ROW_TILES = D_MODEL // LANES
assert ROW_TILES == SUBLANES
VMEM_LIMIT = 56 * 1024 * 1024

QK = GLA_HEADS * GLA_DK
VW = GLA_HEADS * GLA_DV
OFF_XA = 0
OFF_GA = OFF_XA + D_MODEL
OFF_Q = OFF_GA + D_MODEL
OFF_K = OFF_Q + QK
OFF_V = OFF_K + QK
OFF_GO = OFF_V + VW
OFF_GLR = OFF_GO + VW
OFF_MA = OFF_GLR + LANES
OFF_MB = OFF_MA + D_MODEL
IN_PACKED = OFF_MB + D_MODEL

TS = 256
TR = 512
BLK = 256
TC = 256
DMA_PRIORITIES = 2
INVERT_STEPS = 128


def _dot(a, b):
    return jnp.dot(a, b, preferred_element_type=F32)


def _softplus(x):
    return jnp.maximum(x, 0.0) + jnp.log1p(jnp.exp(-jnp.abs(x)))


def _layer_norm(x, g, b):
    mu = jnp.mean(x, axis=-1, keepdims=True)
    xc = x - mu
    var = jnp.mean(xc * xc, axis=-1, keepdims=True)
    return xc * lax.rsqrt(var + LN_EPS) * g + b


def _mixer_kernel(x_ref, win_ref, convw_ref, convb_ref, wri_ref, br_ref, bi_ref, lam_ref,
                  wg_ref, bg_ref, gnorm_ref, wout_ref, ln1g_ref, ln1b_ref, wrt_ref, brt_ref,
                  h1_ref, logits_ref,
                  xa_buf, a_buf, u_buf, y_buf, q_buf, k_buf, v_buf, lg_buf, o_buf, hc_ref, st_ref):
    s = pl.program_id(1)

    @pl.when(s == 0)
    def _():
        xa_buf[0:SUBLANES, :] = jnp.zeros((SUBLANES, D_MODEL), F32)
        hc_ref[...] = jnp.zeros_like(hc_ref)
        st_ref[...] = jnp.zeros_like(st_ref)

    xb = x_ref[...].astype(BF16)

    def proj(off, width):
        return _dot(xb, win_ref[:, off:off + width])

    xa_buf[SUBLANES:SUBLANES + TS, :] = proj(OFF_XA, D_MODEL)
    base = SUBLANES - (CONV_WIDTH - 1)
    xc = xa_buf[base:base + TS, :] * convw_ref[0:1, :]
    for kk in range(1, CONV_WIDTH):
        xc = xc + xa_buf[base + kk:base + kk + TS, :] * convw_ref[kk:kk + 1, :]
    xc = xc + convb_ref[...]
    xa_buf[0:SUBLANES, :] = xa_buf[TS:TS + SUBLANES, :]

    neg_c_sp = -LRU_C * _softplus(-lam_ref[...])
    for g in range(LRU_BLOCKS):
        sl = slice(g * LRU_BLOCK, (g + 1) * LRU_BLOCK)
        xg = xc[:, sl]
        ri = _dot(xg.astype(BF16), wri_ref[g])
        r = jax.nn.sigmoid(ri[:, :LRU_BLOCK] + br_ref[:, sl])
        i = jax.nn.sigmoid(ri[:, LRU_BLOCK:] + bi_ref[:, sl])
        log_a = neg_c_sp[:, sl] * r
        a = jnp.exp(log_a)
        a_buf[:, sl] = a
        u_buf[:, sl] = jnp.sqrt(jnp.tanh(-log_a) * (a * a + 1.0)) * (i * xg)

    def scan_body(t, h):
        h = a_buf[pl.ds(t, 1), :] * h + u_buf[pl.ds(t, 1), :]
        u_buf[pl.ds(t, 1), :] = h
        return h

    hc_ref[...] = lax.fori_loop(0, TS, scan_body, hc_ref[...], unroll=8)

    ga = proj(OFF_GA, D_MODEL)
    ma = proj(OFF_MA, D_MODEL)
    y_buf[...] = jax.nn.sigmoid(ma) * (u_buf[...] * jax.nn.gelu(ga))

    q_buf[...] = proj(OFF_Q, QK) * (GLA_DK ** -0.5)
    k_buf[...] = proj(OFF_K, QK)
    v_buf[...] = proj(OFF_V, VW)
    glr = proj(OFF_GLR, LANES)
    z = _dot(glr.astype(BF16), wg_ref[...]) + bg_ref[...]
    lg_buf[...] = (jnp.minimum(z, 0.0) - jnp.log1p(jnp.exp(-jnp.abs(z)))) * (1.0 / GLA_TAU)

    rows = lax.broadcasted_iota(jnp.int32, (GLA_CHUNK, GLA_CHUNK), 0)
    cols = lax.broadcasted_iota(jnp.int32, (GLA_CHUNK, GLA_CHUNK), 1)
    causal = rows >= cols
    tri = causal.astype(BF16)
    mid = GLA_CHUNK // 2 - 1

    def chunk_body(c, carry):
        r0 = pl.multiple_of(c * GLA_CHUNK, GLA_CHUNK)
        lg = lg_buf[pl.ds(r0, GLA_CHUNK), :]
        lg_hi = lg.astype(BF16)
        rem = lg - lg_hi.astype(F32)
        lg_mid = rem.astype(BF16)
        lg_lo = (rem - lg_mid.astype(F32)).astype(BF16)
        bcum = _dot(tri, lg_hi) + _dot(tri, lg_mid) + _dot(tri, lg_lo)
        for h in range(GLA_HEADS):
            ks = slice(h * GLA_DK, (h + 1) * GLA_DK)
            vs = slice(h * GLA_DV, (h + 1) * GLA_DV)
            b = bcum[:, ks]
            qh = q_buf[pl.ds(r0, GLA_CHUNK), ks]
            kh = k_buf[pl.ds(r0, GLA_CHUNK), ks]
            vh = v_buf[pl.ds(r0, GLA_CHUNK), vs].astype(BF16)
            b_last = b[GLA_CHUNK - 1:GLA_CHUNK, :]
            b_mid = b[mid:mid + 1, :]
            st = st_ref[h]
            q_in = (qh * jnp.exp(b)).astype(BF16)
            o_inter = lax.dot_general(q_in, st.astype(BF16), (((1,), (1,)), ((), ())),
                                      preferred_element_type=F32)
            q_t = (qh * jnp.exp(b - b_mid)).astype(BF16)
            k_t = (kh * jnp.exp(b_mid - b)).astype(BF16)
            scores = lax.dot_general(q_t, k_t, (((1,), (1,)), ((), ())), preferred_element_type=F32)
            scores = jnp.where(causal, scores, 0.0).astype(BF16)
            o = o_inter + _dot(scores, vh)
            k_dec = (kh * jnp.exp(b_last - b)).astype(BF16)
            st_ref[h] = st * jnp.exp(b_last) + lax.dot_general(
                vh, k_dec, (((0,), (0,)), ((), ())), preferred_element_type=F32)
            ms = jnp.mean(o * o, axis=-1, keepdims=True)
            o_buf[pl.ds(r0, GLA_CHUNK), vs] = o * lax.rsqrt(ms + RMS_EPS) * gnorm_ref[...]
        return carry

    lax.fori_loop(0, TS // GLA_CHUNK, chunk_body, 0)

    go = proj(OFF_GO, VW)
    mb = proj(OFF_MB, D_MODEL)
    y = y_buf[...] + jax.nn.sigmoid(mb) * (o_buf[...] * (go * jax.nn.sigmoid(go)))

    hpre = DEEPNORM_ALPHA * x_ref[...] + _dot(y.astype(BF16), wout_ref[...])
    h1 = _layer_norm(hpre, ln1g_ref[...], ln1b_ref[...])
    h1_ref[...] = h1
    logits_ref[...] = jnp.dot(h1, wrt_ref[...], preferred_element_type=F32,
                              precision=lax.Precision.HIGHEST) + brt_ref[...]


def _mixer(x, w_cat, conv_w, conv_b, w_ri, b_r, b_i, lam, w_g, b_g, gnorm, w_out, ln1_g, ln1_b, w_rt, b_rt):
    bsz, seq, d = x.shape
    const2 = lambda shape: pl.BlockSpec(shape, lambda b, s: (0, 0))
    const3 = lambda shape: pl.BlockSpec(shape, lambda b, s: (0, 0, 0))
    return pl.pallas_call(
        _mixer_kernel,
        grid=(bsz, seq // TS),
        in_specs=[
            pl.BlockSpec((None, TS, d), lambda b, s: (b, s, 0)),
            const2((d, IN_PACKED)),
            const2((CONV_WIDTH, d)), const2((1, d)),
            const3((LRU_BLOCKS, LRU_BLOCK, 2 * LRU_BLOCK)), const2((1, d)), const2((1, d)), const2((1, d)),
            const2((LANES, QK)), const2((1, QK)), const2((1, GLA_DV)),
            const2((d, d)), const2((1, d)), const2((1, d)),
            const2((d, LANES)), const2((1, LANES)),
        ],
        out_specs=[
            pl.BlockSpec((None, TS, d), lambda b, s: (b, s, 0)),
            pl.BlockSpec((None, TS, LANES), lambda b, s: (b, s, 0)),
        ],
        out_shape=[
            jax.ShapeDtypeStruct((bsz, seq, d), F32),
            jax.ShapeDtypeStruct((bsz, seq, LANES), F32),
        ],
        scratch_shapes=[
            pltpu.VMEM((TS + SUBLANES, d), F32),
            pltpu.VMEM((TS, d), F32),
            pltpu.VMEM((TS, d), F32),
            pltpu.VMEM((TS, d), F32),
            pltpu.VMEM((TS, QK), F32),
            pltpu.VMEM((TS, QK), F32),
            pltpu.VMEM((TS, VW), F32),
            pltpu.VMEM((TS, QK), F32),
            pltpu.VMEM((TS, VW), F32),
            pltpu.VMEM((1, d), F32),
            pltpu.VMEM((GLA_HEADS, GLA_DV, GLA_DK), F32),
        ],
        compiler_params=pltpu.CompilerParams(
            dimension_semantics=("arbitrary", "arbitrary"), vmem_limit_bytes=VMEM_LIMIT),
        name="mixer",
    )(x, w_cat, conv_w, conv_b, w_ri, b_r, b_i, lam, w_g, b_g, gnorm, w_out, ln1_g, ln1_b, w_rt, b_rt)


def _router_kernel(logits_ref, eidx_ref, gate_ref, rank_ref, cnt_ref, carry_ref):
    i = pl.program_id(0)

    @pl.when(i == 0)
    def _():
        carry_ref[...] = jnp.zeros_like(carry_ref)

    lane = lax.broadcasted_iota(jnp.int32, (TR, LANES), 1)
    work = logits_ref[...]
    vals, idxs, sels = [], [], []
    for _ in range(TOP_K):
        m = jnp.max(work, axis=-1, keepdims=True)
        idx = jnp.min(jnp.where(work == m, lane, LANES), axis=-1, keepdims=True)
        sel = lane == idx
        vals.append(m)
        idxs.append(idx)
        sels.append(sel)
        work = jnp.where(sel, -jnp.inf, work)

    es = [jnp.exp(v - vals[0]) for v in vals]
    denom = es[0] + es[1] + es[2] + es[3]

    member = jnp.zeros((TR, LANES), F32)
    for sel in sels:
        member = member + sel.astype(F32)
    rows = lax.broadcasted_iota(jnp.int32, (TR, TR), 0)
    cols = lax.broadcasted_iota(jnp.int32, (TR, TR), 1)
    strict = (rows > cols).astype(BF16)
    before = _dot(strict, member.astype(BF16)) + carry_ref[0:1, :]

    eidx = jnp.zeros((TR, LANES), jnp.int32)
    gate = jnp.zeros((TR, LANES), F32)
    rank = jnp.zeros((TR, LANES), jnp.int32)
    for kk in range(TOP_K):
        rk = jnp.sum(jnp.where(sels[kk], before, 0.0), axis=-1, keepdims=True).astype(jnp.int32)
        eidx = jnp.where(lane == kk, idxs[kk], eidx)
        gate = jnp.where(lane == kk, es[kk] / denom, gate)
        rank = jnp.where(lane == kk, rk, rank)
    eidx_ref[...] = eidx
    gate_ref[...] = gate
    rank_ref[...] = rank

    total = carry_ref[0:1, :] + jnp.sum(member, axis=0, keepdims=True)
    carry_ref[...] = jnp.broadcast_to(total, carry_ref.shape)
    cnt_ref[...] = jnp.broadcast_to(total, cnt_ref.shape).astype(jnp.int32)


def _router(logits):
    t = logits.shape[0]
    tile = pl.BlockSpec((TR, LANES), lambda i: (i, 0))
    return pl.pallas_call(
        _router_kernel,
        grid=(t // TR,),
        in_specs=[tile],
        out_specs=[tile, tile, tile, pl.BlockSpec((SUBLANES, LANES), lambda i: (0, 0))],
        out_shape=[
            jax.ShapeDtypeStruct((t, LANES), jnp.int32),
            jax.ShapeDtypeStruct((t, LANES), F32),
            jax.ShapeDtypeStruct((t, LANES), jnp.int32),
            jax.ShapeDtypeStruct((SUBLANES, LANES), jnp.int32),
        ],
        scratch_shapes=[pltpu.VMEM((SUBLANES, LANES), F32)],
        compiler_params=pltpu.CompilerParams(dimension_semantics=("arbitrary",)),
        name="router",
    )(logits)


def _invert_kernel(pos_ref, dest_ref):
    n_assign = pos_ref.shape[0]
    t = n_assign // TOP_K
    n_ext = dest_ref.shape[0]
    phase = pl.program_id(0)
    j = pl.program_id(1)
    fill_chunk = n_ext // INVERT_STEPS
    scat_chunk = n_assign // INVERT_STEPS

    @pl.when(phase == 0)
    def _():
        base = j * fill_chunk
        for r in range(fill_chunk):
            dest_ref[base + r] = n_assign + ((base + r) & (BLK - 1))

    @pl.when(phase == 1)
    def _():
        base = j * scat_chunk
        tok0 = j * (scat_chunk // TOP_K)
        for r in range(scat_chunk):
            dest_ref[BLK + pos_ref[base + r]] = (r % TOP_K) * t + tok0 + r // TOP_K


def _invert(pos, n_ext):
    assert n_ext % INVERT_STEPS == 0 and pos.shape[0] % (INVERT_STEPS * TOP_K) == 0
    return pl.pallas_call(
        _invert_kernel,
        grid=(2, INVERT_STEPS),
        in_specs=[pl.BlockSpec(memory_space=pltpu.SMEM)],
        out_specs=pl.BlockSpec(memory_space=pltpu.SMEM),
        out_shape=jax.ShapeDtypeStruct((n_ext,), jnp.int32),
        compiler_params=pltpu.CompilerParams(dimension_semantics=("arbitrary", "arbitrary")),
        name="invert",
    )(pos)


def _row_copy(src, src_row, dst, dst_row, sem):
    return pltpu.make_async_copy(src.at[pl.ds(src_row * ROW_TILES, ROW_TILES), :],
                                 dst.at[pl.ds(dst_row * ROW_TILES, ROW_TILES), :], sem)


def _expert_kernel(be_ref, nused_ref, dest_ref,
                   h1_hbm, wup_ref, bup_ref, wdn_ref, bdn_ref,
                   ycomb_hbm,
                   xbuf0, xbuf1, ybuf0, ybuf1, xb_tmp, gsem, ssem, wup_bf, wdn_bf):
    i = pl.program_id(0)
    n_used = nused_ref[0]
    n_blocks = be_ref.shape[0]
    t = h1_hbm.shape[0] // ROW_TILES
    xbufs = (xbuf0, xbuf1)
    ybufs = (ybuf0, ybuf1)

    def gather_rows(block, parity):
        base = (block + 1) * BLK
        for r in range(BLK):
            tok = dest_ref[base + r] & (t - 1)
            _row_copy(h1_hbm, tok, xbufs[parity], r, gsem.at[parity]).start(priority=r % DMA_PRIORITIES)

    def scatter_rows(block, parity):
        base = (block + 1) * BLK
        for r in range(BLK):
            _row_copy(ybufs[parity], r, ycomb_hbm, dest_ref[base + r], ssem.at[parity]).start(
                priority=r % DMA_PRIORITIES)

    def wait_gather(parity):
        pltpu.make_async_copy(h1_hbm.at[pl.ds(0, BLK * ROW_TILES), :], xbufs[parity], gsem.at[parity]).wait()

    def wait_scatter(parity):
        pltpu.make_async_copy(ybufs[parity], ycomb_hbm.at[pl.ds(0, BLK * ROW_TILES), :], ssem.at[parity]).wait()

    @pl.when(i == 0)
    def _():
        ybuf1[...] = jnp.zeros_like(ybuf1)
        gather_rows(0, 0)

    first_of_expert = jnp.logical_or(i == 0, be_ref[jnp.minimum(i, n_blocks - 1)] != be_ref[jnp.maximum(i - 1, 0)])

    @pl.when(jnp.logical_and(i < n_used, first_of_expert))
    def _():
        wup_bf[...] = wup_ref[...].astype(BF16)
        wdn_bf[...] = wdn_ref[...].astype(BF16)

    def step(parity):
        other = 1 - parity
        wait_gather(parity)

        @pl.when(i >= 1)
        def _():
            wait_scatter(parity)

        for j in range(ROW_TILES):
            xb_tmp[:, j * LANES:(j + 1) * LANES] = xbufs[parity][pl.ds(j, BLK, stride=ROW_TILES), :].astype(BF16)
        gather_rows(jnp.minimum(i + 1, n_blocks - 1), other)
        scatter_rows(i - 1, other)
        hdn = _dot(xb_tmp[...], wup_bf[...]) + bup_ref[...]
        gate_h = jnp.minimum(hdn[:, :D_FF], SWIGLU_LIMIT)
        up_h = jnp.clip(hdn[:, D_FF:], -SWIGLU_LIMIT, SWIGLU_LIMIT)
        act = (up_h + 1.0) * gate_h * jax.nn.sigmoid(SWIGLU_ALPHA * gate_h)
        y = _dot(act.astype(BF16), wdn_bf[...]) + bdn_ref[...]
        for j in range(ROW_TILES):
            ybufs[parity][pl.ds(j, BLK, stride=ROW_TILES), :] = y[:, j * LANES:(j + 1) * LANES]

    def drain(parity):
        other = 1 - parity
        wait_gather(parity)
        wait_scatter(parity)
        scatter_rows(i - 1, other)
        wait_scatter(other)

    for parity in range(2):
        @pl.when(jnp.logical_and(i < n_used, i % 2 == parity))
        def _():
            step(parity)

        @pl.when(jnp.logical_and(i == n_used, i % 2 == parity))
        def _():
            drain(parity)


def _experts(block_expert, n_used, dest_ext, h1_tiles, w_up, b_up, w_down, b_down):
    n_blocks = block_expert.shape[0]
    t = h1_tiles.shape[0] // ROW_TILES
    d = D_MODEL
    last = n_blocks - 1
    wmap = lambda i, be, nu, de: (be[jnp.minimum(i, last)], 0, 0)
    grid_spec = pltpu.PrefetchScalarGridSpec(
        num_scalar_prefetch=3,
        grid=(n_blocks + 1,),
        in_specs=[
            pl.BlockSpec(memory_space=pl.ANY),
            pl.BlockSpec((None, d, 2 * D_FF), wmap),
            pl.BlockSpec((None, 1, 2 * D_FF), wmap),
            pl.BlockSpec((None, D_FF, d), wmap),
            pl.BlockSpec((None, 1, d), wmap),
        ],
        out_specs=pl.BlockSpec(memory_space=pl.ANY),
        scratch_shapes=[
            pltpu.VMEM((BLK * ROW_TILES, LANES), F32), pltpu.VMEM((BLK * ROW_TILES, LANES), F32),
            pltpu.VMEM((BLK * ROW_TILES, LANES), F32), pltpu.VMEM((BLK * ROW_TILES, LANES), F32),
            pltpu.VMEM((BLK, d), BF16),
            pltpu.SemaphoreType.DMA((2,)),
            pltpu.SemaphoreType.DMA((2,)),
            pltpu.VMEM((d, 2 * D_FF), BF16),
            pltpu.VMEM((D_FF, d), BF16),
        ],
    )
    return pl.pallas_call(
        _expert_kernel,
        grid_spec=grid_spec,
        out_shape=jax.ShapeDtypeStruct(((TOP_K * t + BLK) * ROW_TILES, LANES), F32),
        compiler_params=pltpu.CompilerParams(
            dimension_semantics=("arbitrary",), vmem_limit_bytes=VMEM_LIMIT),
        name="experts",
    )(block_expert, n_used, dest_ext, h1_tiles, w_up, b_up, w_down, b_down)


def _final_kernel(y0_ref, y1_ref, y2_ref, y3_ref, gate_ref, h1_ref, p_ref, ln2g_ref, ln2b_ref,
                  wple_ref, wpg_ref, bpg_ref, out_ref):
    g = gate_ref[...]
    parts = []
    for j in range(ROW_TILES):
        mj = y0_ref[pl.ds(j, TC, stride=ROW_TILES), :] * g[:, 0:1]
        for kk, y_ref in enumerate((y1_ref, y2_ref, y3_ref), start=1):
            mj = mj + y_ref[pl.ds(j, TC, stride=ROW_TILES), :] * g[:, kk:kk + 1]
        parts.append(mj)
    m = jnp.concatenate(parts, axis=1)
    h2 =_layer_norm(DEEPNORM_ALPHA * h1_ref[...] + m, ln2g_ref[...], ln2b_ref[...])
    gate = jax.nn.sigmoid(_dot(h2.astype(BF16), wpg_ref[...]) + bpg_ref[...])
    out_ref[...] = h2 + gate * _dot(p_ref[...].astype(BF16), wple_ref[...])


def _final(ycomb, gates, h1, p, ln2_g, ln2_b, w_ple, w_pg, b_pg):
    t, d = h1.shape
    n_tiles = t // TC
    const = lambda shape: pl.BlockSpec(shape, lambda i: (0, 0))
    ycomb_k = lambda kk: pl.BlockSpec((TC * ROW_TILES, LANES), lambda i: (kk * n_tiles + i, 0))
    return pl.pallas_call(
        _final_kernel,
        grid=(n_tiles,),
        in_specs=[
            ycomb_k(0), ycomb_k(1), ycomb_k(2), ycomb_k(3),
            pl.BlockSpec((TC, LANES), lambda i: (i, 0)),
            pl.BlockSpec((TC, d), lambda i: (i, 0)),
            pl.BlockSpec((TC, PLE_DIM), lambda i: (i, 0)),
            const((1, d)), const((1, d)),
            const((PLE_DIM, d)), const((d, d)), const((1, d)),
        ],
        out_specs=pl.BlockSpec((TC, d), lambda i: (i, 0)),
        out_shape=jax.ShapeDtypeStruct((t, d), F32),
        compiler_params=pltpu.CompilerParams(
            dimension_semantics=("arbitrary",), vmem_limit_bytes=VMEM_LIMIT),
        name="final",
    )(ycomb, ycomb, ycomb, ycomb, gates, h1, p, ln2_g, ln2_b, w_ple, w_pg, b_pg)


def kernel(x, p, w_in, conv_w, conv_b, lru_w_r, lru_b_r, lru_w_i, lru_b_i, lru_lambda, gla_w_gate, gla_b_gate, gla_norm_g, w_out, ln1_g, ln1_b, w_router, b_router, w_up, b_up, w_down, b_down, ln2_g, ln2_b, w_ple, w_ple_gate, b_ple_gate):
    bsz, seq, d = x.shape
    t = bsz * seq
    h = x
    for li in range(w_in.shape[0]):
        row = lambda v: v[li].reshape(1, -1)
        w = w_in[li]
        glr_end = OFF_GLR + GLA_RANK
        w_cat = jnp.concatenate(
            [w[:, :glr_end], jnp.zeros((d, LANES - GLA_RANK), w.dtype), w[:, glr_end:]],
            axis=1).astype(BF16)
        w_ri = jnp.concatenate([lru_w_r[li], lru_w_i[li]], axis=-1).astype(BF16)
        w_g = jnp.concatenate(
            [gla_w_gate[li], jnp.zeros((LANES - GLA_RANK, QK), F32)], axis=0).astype(BF16)
        w_rt = jnp.concatenate([w_router[li], jnp.zeros((d, LANES - N_EXPERTS), F32)], axis=1)
        b_rt = jnp.concatenate(
            [b_router[li], jnp.full((LANES - N_EXPERTS,), -jnp.inf, F32)]).reshape(1, LANES)

        h1, logits = _mixer(h, w_cat, conv_w[li], row(conv_b), w_ri, row(lru_b_r), row(lru_b_i),
                            row(lru_lambda), w_g, row(gla_b_gate), row(gla_norm_g),
                            w_out[li].astype(BF16), row(ln1_g), row(ln1_b), w_rt, b_rt)
        h1 = h1.reshape(t, d)

        eidx, gates, rank, counts = _router(logits.reshape(t, LANES))

        counts = counts[0, :N_EXPERTS]
        padded = (counts + BLK - 1) // BLK * BLK
        padded_end = jnp.cumsum(padded)
        padded_start = padded_end - padded
        e = eidx[:, :TOP_K]
        onehot = e[:, :, None] == jnp.arange(N_EXPERTS, dtype=jnp.int32)
        pos = jnp.sum(jnp.where(onehot, padded_start, 0), axis=-1) + rank[:, :TOP_K]
        pos = pos.reshape(-1).astype(jnp.int32)
        n_slots = t * TOP_K + N_EXPERTS * BLK
        n_blocks = n_slots // BLK
        block_row0 = jnp.arange(n_blocks, dtype=jnp.int32)[:, None] * BLK
        block_expert = jnp.minimum(
            jnp.sum((padded_end[None, :] <= block_row0).astype(jnp.int32), axis=-1), N_EXPERTS - 1)
        n_used = (padded_end[-1:] // BLK).astype(jnp.int32)

        dest_ext = _invert(pos, (n_blocks + 1) * BLK)
        ycomb = _experts(block_expert, n_used, dest_ext, h1.reshape(t * ROW_TILES, LANES),
                         w_up[li], b_up[li].reshape(N_EXPERTS, 1, 2 * D_FF),
                         w_down[li], b_down[li].reshape(N_EXPERTS, 1, d))

        out = _final(ycomb, gates, h1, p[li].reshape(t, PLE_DIM), row(ln2_g), row(ln2_b),
                     w_ple[li].astype(BF16), w_ple_gate[li].astype(BF16), row(b_ple_gate))
        h = out.reshape(bsz, seq, d)
    return h
```
